```python
import math
import jax, jax.numpy as jnp
from jax import lax
import numpy as np

D_MODEL = 1024
BATCH = 16
SEQ = 2048
DEPTH = 2

N_META = 16
ROPE_THETA = 10000.0
NORM_EPS = 1e-6

ATT_HEAD_DIM = 64
ATT_HEADS = D_MODEL // (2 * ATT_HEAD_DIM)
ATT_QK_W = ATT_HEADS * 2 * ATT_HEAD_DIM
ATT_V_DIM = 2 * ATT_HEAD_DIM
ATT_W = ATT_HEADS * ATT_V_DIM
ATT_BLOCK = 128

RWKV_HEAD_DIM = 64
RWKV_HEADS = D_MODEL // RWKV_HEAD_DIM
RWKV_W = RWKV_HEADS * RWKV_HEAD_DIM
RWKV_DECAY_RANK = 64
RWKV_A_RANK = 64
RWKV_GN_EPS = 64e-5
RWKV_IN_W = 4 * RWKV_W + RWKV_DECAY_RANK + RWKV_A_RANK
RWKV_SPLITS = (RWKV_W, 2 * RWKV_W, 3 * RWKV_W, 4 * RWKV_W, 4 * RWKV_W + RWKV_DECAY_RANK)

HGRN_EXPAND = 128
HGRN_HEADS = D_MODEL // HGRN_EXPAND
HGRN_W = HGRN_HEADS * HGRN_EXPAND
HGRN_V_DIM = HGRN_W // HGRN_HEADS
HGRN_CHUNK = 64

N_BRANCH = 3
IN_WIDTHS = (ATT_QK_W, ATT_QK_W, ATT_W, ATT_W, RWKV_IN_W, HGRN_W, HGRN_W, HGRN_W, HGRN_W, N_BRANCH * D_MODEL)
IN_SPLITS = tuple(sum(IN_WIDTHS[:i + 1]) for i in range(len(IN_WIDTHS) - 1))
IN_W = sum(IN_WIDTHS)

kernel_name = "hybrid_diffattn_rwkv7_hgrn2_gated_merge"


def rms_norm(x, w, eps=NORM_EPS):
    xf = x.astype(jnp.float32)
    y = xf * lax.rsqrt(jnp.mean(xf * xf, axis=-1, keepdims=True) + eps)
    return (y * w.astype(jnp.float32)).astype(x.dtype)


def head_layer_norm(x, w, b, eps):
    xf = x.astype(jnp.float32)
    mu = jnp.mean(xf, axis=-1, keepdims=True)
    var = jnp.mean(jnp.square(xf - mu), axis=-1, keepdims=True)
    y = (xf - mu) * lax.rsqrt(var + eps)
    return y * w.reshape(x.shape[-2:]).astype(jnp.float32) + b.reshape(x.shape[-2:]).astype(jnp.float32)


def rotary_tables(n_pos, dim, dtype):
    inv = 1.0 / (ROPE_THETA ** (jnp.arange(0, dim, 2, dtype=jnp.float32) / dim))
    ang = jnp.arange(n_pos, dtype=jnp.float32)[:, None] * inv[None, :]
    ang = jnp.concatenate([ang, ang], axis=-1)
    return jnp.cos(ang).astype(dtype), jnp.sin(ang).astype(dtype)


def apply_rope(x, cos, sin):
    c = cos[None, :, None, None, :]
    s = sin[None, :, None, None, :]
    x1, x2 = jnp.split(x, 2, axis=-1)
    return x * c + jnp.concatenate([-x2, x1], axis=-1) * s


def token_shift(z):
    return jnp.pad(z[:, :-1], ((0, 0), (1, 0), (0, 0)))


def diff_attention(q, k, v, lam, cos, sin):
    B, L = q.shape[0], q.shape[1]
    pad = (-L) % ATT_BLOCK
    P = L + pad
    q = apply_rope(q, cos, sin)
    k = apply_rope(k, cos, sin)
    q = jnp.pad(jnp.transpose(q, (0, 2, 3, 1, 4)), ((0, 0), (0, 0), (0, 0), (pad, 0), (0, 0)))
    k = jnp.pad(jnp.transpose(k, (0, 2, 3, 1, 4)), ((0, 0), (0, 0), (0, 0), (pad, 0), (0, 0)))
    v = jnp.pad(jnp.transpose(v, (0, 2, 1, 3)), ((0, 0), (0, 0), (pad, 0), (0, 0)))
    key_pos = jnp.arange(P)
    scale = ATT_HEAD_DIM ** -0.5

    def block(n):
        start = n * ATT_BLOCK
        qb = lax.dynamic_slice_in_dim(q, start, ATT_BLOCK, axis=3)
        s = jnp.einsum('bhgqd,bhgkd->bhgqk', qb, k, preferred_element_type=jnp.float32) * scale
        q_pos = start + jnp.arange(ATT_BLOCK)
        allowed = (key_pos[None, :] <= q_pos[:, None]) & (key_pos[None, :] >= pad)
        p = jax.nn.softmax(jnp.where(allowed, s, -1e30), axis=-1)
        w = p[:, :, 0] - lam * p[:, :, 1]
        return jnp.einsum('bhqk,bhkv->bhqv', w.astype(v.dtype), v)

    o = lax.map(block, jnp.arange(P // ATT_BLOCK))
    o = jnp.transpose(o, (1, 0, 3, 2, 4)).reshape(B, P, ATT_HEADS, ATT_V_DIM)
    return o[:, pad:]


def rwkv7_recurrence(r, w, k, v, a, b):
    B, L, H, N = r.shape

    def step(S, inp):
        r_t, w_t, k_t, v_t, a_t, b_t = inp
        sa = jnp.einsum('bhvk,bhk->bhv', S, a_t)
        S = S * w_t[:, :, None, :] + sa[..., None] * b_t[:, :, None, :] + v_t[..., None] * k_t[:, :, None, :]
        return S, jnp.einsum('bhvk,bhk->bhv', S, r_t)

    xs = tuple(jnp.moveaxis(t.astype(jnp.float32), 1, 0) for t in (r, w, k, v, a, b))
    _, o = lax.scan(step, jnp.zeros((B, H, N, N), jnp.float32), xs)
    return jnp.moveaxis(o, 0, 1)


def hgrn2_chunked(q, k, v, log_f):
    B, L, H, K = q.shape
    V = v.shape[-1]
    C = HGRN_CHUNK
    pad = (-L) % C
    P = L + pad
    n = P // C

    def chunks(t):
        t = jnp.pad(t.astype(jnp.float32), ((0, 0), (pad, 0), (0, 0), (0, 0)))
        return t.reshape(B, n, C, H, t.shape[-1]).transpose(1, 0, 3, 2, 4)

    causal = jnp.tril(jnp.ones((C, C), dtype=bool))

    def step(S, inp):
        qc, kc, vc, gc = inp
        lam = jnp.cumsum(gc, axis=2)
        o_inter = jnp.einsum('bhtk,bhkv->bhtv', qc * jnp.exp(lam), S)
        rel = lam[:, :, :, None, :] - lam[:, :, None, :, :]
        decay = jnp.exp(jnp.where(causal[:, :, None], rel, -jnp.inf))
        att = jnp.einsum('bhtk,bhsk,bhtsk->bhts', qc, kc, decay)
        o = o_inter + jnp.einsum('bhts,bhsv->bhtv', att, vc)
        lam_end = lam[:, :, -1:, :]
        S = jnp.exp(lam_end[:, :, 0, :])[..., None] * S + jnp.einsum('bhsk,bhsv->bhkv', kc * jnp.exp(lam_end - lam), vc)
        return S, o

    _, o = lax.scan(step, jnp.zeros((B, H, K, V), jnp.float32), (chunks(q), chunks(k), chunks(v), chunks(log_f)))
    o = o.transpose(1, 0, 3, 2, 4).reshape(B, P, H, V)
    return o[:, pad:]


def hybrid_layer(h, l, cos, sin, pre_w, post_w, w_in, lq1, lk1, lq2, lk2, att_norm_w,
                 rwkv_mu, rwkv_w0, rwkv_w_up, rwkv_a0, rwkv_a_up, rwkv_k_k, rwkv_k_a, rwkv_r_k,
                 rwkv_gn_w, rwkv_gn_b, hgrn_lb, hgrn_norm_w, w_att_out, w_rwkv_out, w_hgrn_out, w_o):
    B, L, _ = h.shape
    f32 = jnp.float32
    u = rms_norm(h, pre_w)
    z = u @ w_in
    aq, ak, av, ag, rz, hq, hf, hi, hg, mg = jnp.split(z, IN_SPLITS, axis=-1)

    lam_init = 0.8 - 0.6 * math.exp(-0.3 * l)
    lam = (jnp.exp(jnp.sum(lq1.astype(f32) * lk1.astype(f32))) - jnp.exp(jnp.sum(lq2.astype(f32) * lk2.astype(f32))) + lam_init)
    o_att = diff_attention(aq.reshape(B, L, ATT_HEADS, 2, ATT_HEAD_DIM), ak.reshape(B, L, ATT_HEADS, 2, ATT_HEAD_DIM),
                           av.reshape(B, L, ATT_HEADS, ATT_V_DIM), lam, cos, sin)
    o_att = rms_norm(o_att, att_norm_w) * (1.0 - lam_init)
    o_att = (o_att.reshape(B, L, ATT_W) * jax.nn.silu(ag)).astype(h.dtype)

    rz = rz + (token_shift(rz) - rz) * rwkv_mu
    rr, rk, rv, rg, rwd, rad = jnp.split(rz, RWKV_SPLITS, axis=-1)
    w_log = -jax.nn.softplus(-(rwkv_w0 + jnp.tanh(rwd) @ rwkv_w_up)) - 0.5
    decay = jnp.exp(-jnp.exp(w_log.astype(f32)))
    a = jax.nn.sigmoid((rwkv_a0 + rad @ rwkv_a_up).astype(f32))
    heads = lambda t: t.reshape(B, L, RWKV_HEADS, RWKV_HEAD_DIM)
    kk = heads((rk * rwkv_k_k).astype(f32))
    kk = kk / jnp.maximum(jnp.sqrt(jnp.sum(kk * kk, axis=-1, keepdims=True)), 1e-12)
    rk = rk.astype(f32) * (1.0 + (a - 1.0) * rwkv_k_a.astype(f32))
    r_h, k_h, v_h, a_h = heads(rr.astype(f32)), heads(rk), heads(rv.astype(f32)), heads(a)
    o = rwkv7_recurrence(r_h, heads(decay), k_h, v_h, -kk, kk * a_h)
    o = head_layer_norm(o, rwkv_gn_w, rwkv_gn_b, RWKV_GN_EPS)
    o = o + jnp.sum(r_h * k_h * rwkv_r_k.astype(f32), axis=-1, keepdims=True) * v_h
    o_rwkv = (o.reshape(B, L, RWKV_W) * jax.nn.silu(rg.astype(f32))).astype(h.dtype)

    lb = hgrn_lb.reshape(HGRN_HEADS, HGRN_EXPAND)
    f_gate = lb + (1.0 - lb) * jax.nn.sigmoid(hf.astype(f32).reshape(B, L, HGRN_HEADS, HGRN_EXPAND))
    o = hgrn2_chunked(jax.nn.silu(hq).reshape(B, L, HGRN_HEADS, HGRN_EXPAND), 1.0 - f_gate,
                      hi.reshape(B, L, HGRN_HEADS, HGRN_V_DIM), jnp.log(f_gate))
    o = rms_norm(o, hgrn_norm_w) * jax.nn.silu(hg.astype(f32)).reshape(B, L, HGRN_HEADS, HGRN_V_DIM)
    o_hgrn = o.reshape(B, L, HGRN_W).astype(h.dtype)

    g_att, g_rwkv, g_hgrn = jnp.split(jax.nn.sigmoid(mg), N_BRANCH, axis=-1)
    y = g_att * (o_att @ w_att_out) + g_rwkv * (o_rwkv @ w_rwkv_out) + g_hgrn * (o_hgrn @ w_hgrn_out)
    return h + rms_norm(y @ w_o, post_w)


def setup_inputs(seed: int = 0) -> dict:
    key = jax.random.key(seed)
    ks = jax.random.split(key, 26)
    f32 = jnp.float32

    def nrm(k, shape, scale):
        return jax.random.normal(k, shape, f32) * scale

    return {
        "x": nrm(ks[0], (BATCH, SEQ, D_MODEL), 1.0),
        "meta_tokens": nrm(ks[1], (N_META, D_MODEL), 1.0),
        "pre_norm_w": 1.0 + nrm(ks[2], (DEPTH, D_MODEL), 0.05),
        "post_norm_w": 1.0 + nrm(ks[3], (DEPTH, D_MODEL), 0.05),
        "w_in": nrm(ks[4], (DEPTH, D_MODEL, IN_W), D_MODEL ** -0.5),
        "lambda_q1": nrm(ks[5], (DEPTH, ATT_HEAD_DIM), 0.1),
        "lambda_k1": nrm(ks[6], (DEPTH, ATT_HEAD_DIM), 0.1),
        "lambda_q2": nrm(ks[7], (DEPTH, ATT_HEAD_DIM), 0.1),
        "lambda_k2": nrm(ks[8], (DEPTH, ATT_HEAD_DIM), 0.1),
        "att_norm_w": 1.0 + nrm(ks[9], (DEPTH, ATT_V_DIM), 0.05),
        "rwkv_mu": jax.random.uniform(ks[10], (DEPTH, RWKV_IN_W), f32),
        "rwkv_w0": jax.random.uniform(ks[11], (DEPTH, RWKV_W), f32, minval=-6.0, maxval=1.0),
        "rwkv_w_up": nrm(ks[12], (DEPTH, RWKV_DECAY_RANK, RWKV_W), 0.5 * RWKV_DECAY_RANK ** -0.5),
        "rwkv_a0": nrm(ks[13], (DEPTH, RWKV_W), 0.1),
        "rwkv_a_up": nrm(ks[14], (DEPTH, RWKV_A_RANK, RWKV_W), 0.5 * RWKV_A_RANK ** -0.5),
        "rwkv_k_k": 0.85 + nrm(ks[15], (DEPTH, RWKV_W), 0.05),
        "rwkv_k_a": 1.0 + nrm(ks[16], (DEPTH, RWKV_W), 0.05),
        "rwkv_r_k": nrm(ks[17], (DEPTH, RWKV_HEADS, RWKV_HEAD_DIM), 0.1),
        "rwkv_gn_w": 1.0 + nrm(ks[18], (DEPTH, RWKV_W), 0.05),
        "rwkv_gn_b": nrm(ks[19], (DEPTH, RWKV_W), 0.01),
        "hgrn_lower_bounds": nrm(ks[20], (DEPTH, HGRN_W), 0.1),
        "hgrn_norm_w": 1.0 + nrm(ks[21], (DEPTH, HGRN_V_DIM), 0.05),
        "w_att_out": nrm(ks[22], (DEPTH, ATT_W, D_MODEL), ATT_W ** -0.5),
        "w_rwkv_out": nrm(ks[23], (DEPTH, RWKV_W, D_MODEL), RWKV_W ** -0.5),
        "w_hgrn_out": nrm(ks[24], (DEPTH, HGRN_W, D_MODEL), HGRN_W ** -0.5),
        "w_o": nrm(ks[25], (DEPTH, D_MODEL, D_MODEL), D_MODEL ** -0.5),
    }


def reference(x, meta_tokens, pre_norm_w, post_norm_w, w_in, lambda_q1, lambda_k1, lambda_q2, lambda_k2,
              att_norm_w, rwkv_mu, rwkv_w0, rwkv_w_up, rwkv_a0, rwkv_a_up, rwkv_k_k, rwkv_k_a, rwkv_r_k,
              rwkv_gn_w, rwkv_gn_b, hgrn_lower_bounds, hgrn_norm_w, w_att_out, w_rwkv_out, w_hgrn_out, w_o):
    B = x.shape[0]
    meta = jnp.broadcast_to(meta_tokens[None].astype(x.dtype), (B, N_META, D_MODEL))
    h = jnp.concatenate([meta, x], axis=1)
    L = h.shape[1]
    cos, sin = rotary_tables(L, ATT_HEAD_DIM, x.dtype)
    lbs = jnp.cumsum(jax.nn.softmax(hgrn_lower_bounds.astype(jnp.float32), axis=0), axis=0)
    lbs = lbs - lbs[0]
    for l in range(DEPTH):
        h = hybrid_layer(h, l, cos, sin, pre_norm_w[l], post_norm_w[l], w_in[l],
                         lambda_q1[l], lambda_k1[l], lambda_q2[l], lambda_k2[l], att_norm_w[l],
                         rwkv_mu[l], rwkv_w0[l], rwkv_w_up[l], rwkv_a0[l], rwkv_a_up[l], rwkv_k_k[l],
                         rwkv_k_a[l], rwkv_r_k[l], rwkv_gn_w[l], rwkv_gn_b[l], lbs[l], hgrn_norm_w[l],
                         w_att_out[l], w_rwkv_out[l], w_hgrn_out[l], w_o[l])
    return h[:, N_META:]
```

```python
import functools
import math

import numpy as np
import jax
import jax.numpy as jnp
from jax import lax
from jax.experimental import pallas as pl
from jax.experimental.pallas import tpu as pltpu

F32 = jnp.float32
BF16 = jnp.bfloat16

D_MODEL = 1024
N_META = 16
LANES = 128
ROW_TILE = 128
PAD = ROW_TILE - N_META
ROPE_THETA = 10000.0
NORM_EPS = 1e-6
NEG_BIG = -1e30

ATT_HEAD_DIM = 64
ATT_HEADS = 8
RWKV_HEADS = 16
RWKV_N = 64
RWKV_GN_EPS = 64e-5
RWKV_LORA = 128
HGRN_HEADS = 8
HGRN_K = 128
CHUNK = 64
HGRN_LEVELS = 6

Z_W = 15 * D_MODEL
COL_ATT, COL_RWKV, COL_HGRN, COL_MG = 0, 4 * D_MODEL, 8 * D_MODEL, 12 * D_MODEL
IN_TN = 512
VMEM_LIMIT = 56 * 1024 * 1024


def _dot(a, b):
    return jnp.dot(a.astype(BF16), b.astype(BF16), preferred_element_type=F32)


def _dot_nt(a, b):
    return lax.dot_general(a.astype(BF16), b.astype(BF16), (((1,), (1,)), ((), ())),
                           preferred_element_type=F32)


def _dot_tn(a, b):
    return lax.dot_general(a.astype(BF16), b.astype(BF16), (((0,), (0,)), ((), ())),
                           preferred_element_type=F32)


def _split(x):
    hi = x.astype(BF16)
    lo = (x - hi.astype(F32)).astype(BF16)
    return hi, lo


def _dot_x3(a, b):
    ah, al = _split(a)
    bh, bl = _split(b)
    d = functools.partial(jnp.dot, preferred_element_type=F32)
    return d(ah, bh) + d(al, bh) + d(ah, bl)


def _dot_exact_lhs(p_bf16, x):
    xh, xl = _split(x)
    d = functools.partial(jnp.dot, preferred_element_type=F32)
    return d(p_bf16, xh) + d(p_bf16, xl)


def _sigmoid(x):
    return 1.0 / (1.0 + jnp.exp(-x))


def _silu(x):
    return x * _sigmoid(x)


def _inproj_kernel(h_ref, prew_ref, w_ref, wlh_ref, wll_ref, cos_ref, sin_ref, mu_ref, mul_ref,
                   z_ref, zl_ref, u_ref, *, tn):
    j = pl.program_id(1)
    rows = h_ref.shape[0]

    def shift_mix(acc, mu):
        prev = pltpu.roll(acc, 1, 0)
        row = lax.broadcasted_iota(jnp.int32, acc.shape, 0)
        prev = jnp.where(row == 0, 0.0, prev)
        return acc + (prev - acc) * mu

    @pl.when(j == 0)
    def _():
        x = h_ref[...]
        ms = jnp.mean(x * x, axis=-1, keepdims=True)
        u = x * lax.rsqrt(ms + NORM_EPS) * prew_ref[...]
        u_ref[...] = u.astype(BF16)
        uh, ul = _split(u)
        d = functools.partial(jnp.dot, preferred_element_type=F32)
        zl = d(uh, wlh_ref[...]) + d(ul, wlh_ref[...]) + d(uh, wll_ref[...])
        zl_ref[...] = shift_mix(zl, mul_ref[...])

    acc = jnp.dot(u_ref[...], w_ref[...], preferred_element_type=F32)
    n_q, n_qk = D_MODEL // tn, 2 * D_MODEL // tn
    is_rope = j < n_qk
    is_rwkv = (j >= COL_RWKV // tn) & (j < COL_HGRN // tn)

    @pl.when(is_rope)
    def _():
        scale = jnp.where(j < n_q, ATT_HEAD_DIM ** -0.5, 1.0).astype(F32)
        lane = lax.broadcasted_iota(jnp.int32, (rows, LANES), 1)
        first_half = (lane % ATT_HEAD_DIM) < (ATT_HEAD_DIM // 2)
        cos = cos_ref[...] * scale
        sin = jnp.where(first_half, -sin_ref[...], sin_ref[...]) * scale
        for g in range(tn // LANES):
            x = acc[:, g * LANES:(g + 1) * LANES]
            rot = jnp.where(first_half, pltpu.roll(x, LANES - ATT_HEAD_DIM // 2, 1),
                            pltpu.roll(x, ATT_HEAD_DIM // 2, 1))
            z_ref[:, g * LANES:(g + 1) * LANES] = x * cos + rot * sin

    @pl.when(is_rwkv)
    def _():
        z_ref[...] = shift_mix(acc, mu_ref[...])

    @pl.when(jnp.logical_not(is_rope | is_rwkv))
    def _():
        z_ref[...] = acc


def _in_proj(h, pre_w, w_main, wl_hi, wl_lo, cos, sin, mu_main, mu_lora, *, lp):
    t = h.shape[0]
    nb = t // lp
    tn = IN_TN
    return pl.pallas_call(
        functools.partial(_inproj_kernel, tn=tn),
        grid=(nb, Z_W // tn),
        in_specs=[
            pl.BlockSpec((lp, D_MODEL), lambda i, j: (i, 0)),
            pl.BlockSpec((1, D_MODEL), lambda i, j: (0, 0)),
            pl.BlockSpec((D_MODEL, tn), lambda i, j: (0, j)),
            pl.BlockSpec((D_MODEL, RWKV_LORA), lambda i, j: (0, 0)),
            pl.BlockSpec((D_MODEL, RWKV_LORA), lambda i, j: (0, 0)),
            pl.BlockSpec((lp, LANES), lambda i, j: (0, 0)),
            pl.BlockSpec((lp, LANES), lambda i, j: (0, 0)),
            pl.BlockSpec((1, tn), lambda i, j: (0, j)),
            pl.BlockSpec((1, RWKV_LORA), lambda i, j: (0, 0)),
        ],
        out_specs=[
            pl.BlockSpec((lp, tn), lambda i, j: (i, j)),
            pl.BlockSpec((lp, RWKV_LORA), lambda i, j: (i, 0)),
        ],
        out_shape=[jax.ShapeDtypeStruct((t, Z_W), F32), jax.ShapeDtypeStruct((t, RWKV_LORA), F32)],
        scratch_shapes=[pltpu.VMEM((lp, D_MODEL), BF16)],
        compiler_params=pltpu.CompilerParams(dimension_semantics=("parallel", "arbitrary"),
                                             vmem_limit_bytes=VMEM_LIMIT),
        name="in_proj",
    )(h, pre_w, w_main, wl_hi, wl_lo, cos, sin, mu_main, mu_lora)


def _attn_kernel(lq1_ref, lk1_ref, lq2_ref, lk2_ref, nw_ref, q_ref, k_ref, v_ref, g_ref,
                 o_ref, kb_ref, vb_ref, *, lam_init):
    lp = q_ref.shape[0]
    nblk = lp // ROW_TILE
    lam = (jnp.exp(jnp.sum(lq1_ref[...] * lk1_ref[...], keepdims=True))
           - jnp.exp(jnp.sum(lq2_ref[...] * lk2_ref[...], keepdims=True)) + lam_init)
    kb_ref[...] = k_ref[...].astype(BF16)
    vb_ref[...] = v_ref[...].astype(BF16)
    lane = lax.broadcasted_iota(jnp.int32, (ROW_TILE, LANES), 1)
    half1 = lane < ATT_HEAD_DIM
    row_i = lax.broadcasted_iota(jnp.int32, (ROW_TILE, ROW_TILE), 0)
    col_i = lax.broadcasted_iota(jnp.int32, (ROW_TILE, ROW_TILE), 1)
    out_scale = nw_ref[...] * (1.0 - lam_init)

    def q_block(qi, _):
        r0 = pl.multiple_of(qi * ROW_TILE, ROW_TILE)
        qb = q_ref[pl.ds(r0, ROW_TILE), :]
        q1 = jnp.where(half1, qb, 0.0).astype(BF16)
        q2 = jnp.where(half1, 0.0, qb).astype(BF16)

        def kv_block(kj, carry):
            m1, l1, a1, m2, l2, a2 = carry
            c0 = pl.multiple_of(kj * ROW_TILE, ROW_TILE)
            kb = kb_ref[pl.ds(c0, ROW_TILE), :]
            vb = vb_ref[pl.ds(c0, ROW_TILE), :]
            allowed = (c0 + col_i <= r0 + row_i) & (c0 + col_i >= PAD)

            def update(qh, m, l, a):
                s = lax.dot_general(qh, kb, (((1,), (1,)), ((), ())), preferred_element_type=F32)
                s = jnp.where(allowed, s, NEG_BIG)
                m_new = jnp.maximum(m, jnp.max(s, axis=-1, keepdims=True))
                p = jnp.exp(s - m_new)
                alpha = jnp.exp(m - m_new)
                l_new = alpha * l + jnp.sum(p, axis=-1, keepdims=True)
                a_new = alpha * a + jnp.dot(p.astype(BF16), vb, preferred_element_type=F32)
                return m_new, l_new, a_new

            m1, l1, a1 = update(q1, m1, l1, a1)
            m2, l2, a2 = update(q2, m2, l2, a2)
            return m1, l1, a1, m2, l2, a2

        col0 = jnp.full((ROW_TILE, 1), NEG_BIG, F32)
        zc = jnp.zeros((ROW_TILE, 1), F32)
        za = jnp.zeros((ROW_TILE, LANES), F32)
        m1, l1, a1, m2, l2, a2 = lax.fori_loop(0, qi + 1, kv_block, (col0, zc, za, col0, zc, za))
        o = a1 * (1.0 / l1) - lam * (a2 * (1.0 / l2))
        o = o * lax.rsqrt(jnp.mean(o * o, axis=-1, keepdims=True) + NORM_EPS) * out_scale
        o_ref[pl.ds(r0, ROW_TILE), :] = (o * _silu(g_ref[pl.ds(r0, ROW_TILE), :])).astype(BF16)
        return 0

    lax.fori_loop(0, nblk, q_block, 0)


def _attention(z, lq1, lk1, lq2, lk2, norm_w, *, lp, lam_init):
    t = z.shape[0]
    nb = t // lp
    small = lambda n: pl.BlockSpec((1, n), lambda b, h: (0, 0))
    colblk = lambda base: pl.BlockSpec((lp, LANES), lambda b, h: (b, base + h))
    return pl.pallas_call(
        functools.partial(_attn_kernel, lam_init=lam_init),
        grid=(nb, ATT_HEADS),
        in_specs=[small(ATT_HEAD_DIM)] * 4 + [small(LANES)]
                 + [colblk(0), colblk(ATT_HEADS), colblk(2 * ATT_HEADS), colblk(3 * ATT_HEADS)],
        out_specs=pl.BlockSpec((lp, LANES), lambda b, h: (b, h)),
        out_shape=jax.ShapeDtypeStruct((t, D_MODEL), BF16),
        scratch_shapes=[pltpu.VMEM((lp, LANES), BF16), pltpu.VMEM((lp, LANES), BF16)],
        compiler_params=pltpu.CompilerParams(dimension_semantics=("parallel", "parallel"),
                                             vmem_limit_bytes=VMEM_LIMIT),
        name="diff_attn",
    )(lq1, lk1, lq2, lk2, norm_w, z, z, z, z)


def _rwkv_kernel(tril_ref, w0_ref, wup_ref, a0_ref, aup_ref, kk_ref, ka_ref, rk_ref, gnw_ref, gnb_ref,
                 zr_ref, zk_ref, zv_ref, zg_ref, zl_ref, o_ref, ht_ref):
    c = pl.program_id(1)

    @pl.when(c == 0)
    def _():
        ht_ref[...] = jnp.zeros_like(ht_ref)

    n = RWKV_N
    zl = zl_ref[...]
    w_pre = w0_ref[...] + _dot_x3(jnp.tanh(zl[:, :n]), wup_ref[...])
    y = -w_pre
    softplus = jnp.maximum(y, 0.0) + jnp.log(1.0 + jnp.exp(-jnp.abs(y)))
    logw = -jnp.exp(-softplus - 0.5)
    a_all = _sigmoid(a0_ref[...] + _dot_x3(zl[:, n:], aup_ref[...]))
    lam_i_all = _dot_exact_lhs(tril_ref[...], logw)
    r_all, k_all, v_all, g_all = zr_ref[...], zk_ref[...], zv_ref[...], zg_ref[...]
    kk_all = k_all * kk_ref[...]
    k2_all = k_all * (1.0 + (a_all - 1.0) * ka_ref[...])
    rkr_all = r_all * k2_all * rk_ref[...]

    row = lax.broadcasted_iota(jnp.int32, (CHUNK, CHUNK), 0)
    col = lax.broadcasted_iota(jnp.int32, (CHUNK, CHUNK), 1)
    strict, incl = col < row, col <= row
    eye = jnp.where(row == col, 1.0, 0.0).astype(F32)

    outs = []
    for h in range(RWKV_HEADS):
        sl = slice(h * n, (h + 1) * n)
        r, k2, v, a = r_all[:, sl], k2_all[:, sl], v_all[:, sl], a_all[:, sl]
        kk = kk_all[:, sl]
        kk = kk / jnp.maximum(jnp.sqrt(jnp.sum(kk * kk, axis=-1, keepdims=True)), 1e-12)
        lam_i = lam_i_all[:, sl]
        lam_e = lam_i - logw[:, sl]
        lam_end = lam_i[CHUNK - 1:CHUNK, :]
        e_ni = jnp.exp(-lam_i)
        rt = r * jnp.exp(lam_i)
        at = -kk * jnp.exp(lam_e)
        kb = kk * a
        bt = kb * e_ni
        kt = k2 * e_ni
        lhs = jnp.concatenate([at, rt], axis=0)
        mb = _dot_nt(lhs, bt)
        mk = _dot_nt(lhs, kt)
        a_ab = jnp.where(strict, mb[:CHUNK], 0.0)
        a_rb = jnp.where(incl, mb[CHUNK:], 0.0)
        a_ak = jnp.where(strict, mk[:CHUNK], 0.0)
        a_rk = jnp.where(incl, mk[CHUNK:], 0.0)
        av = _dot(jnp.concatenate([a_ak, a_rk], axis=0), v)
        tinv = eye + a_ab
        p = a_ab
        for _ in range(int(math.log2(CHUNK)) - 1):
            p = _dot(p, p)
            tinv = tinv + _dot(tinv, p)
        w_hat = _dot(tinv, at)
        u_hat = _dot(tinv, av[:CHUNK])
        ht = ht_ref[h]
        g2 = _dot_nt(jnp.concatenate([w_hat, rt], axis=0), ht)
        u = g2[:CHUNK] + u_hat
        o = g2[CHUNK:] + _dot(a_rb, u) + av[CHUNK:]
        e_end = jnp.exp(lam_end - lam_i)
        ht_ref[h] = ht * jnp.exp(lam_end) + _dot_tn(jnp.concatenate([u, v], axis=0),
                                                     jnp.concatenate([kb * e_end, k2 * e_end], axis=0))
        mu = jnp.mean(o, axis=-1, keepdims=True)
        var = jnp.mean(jnp.square(o - mu), axis=-1, keepdims=True)
        o = (o - mu) * lax.rsqrt(var + RWKV_GN_EPS) * gnw_ref[:, sl] + gnb_ref[:, sl]
        o = o + jnp.sum(rkr_all[:, sl], axis=-1, keepdims=True) * v
        outs.append(o * _silu(g_all[:, sl]))
    o_ref[...] = jnp.concatenate(outs, axis=1).astype(BF16)


def _rwkv(z, zl, tril, w0, wup, a0, aup, k_k, k_a, r_k, gn_w, gn_b, *, lp):
    t = z.shape[0]
    nb, nc = t // lp, lp // CHUNK
    vec = pl.BlockSpec((1, D_MODEL), lambda b, c: (0, 0))
    lora = pl.BlockSpec((RWKV_N, D_MODEL), lambda b, c: (0, 0))
    zblk = lambda k: pl.BlockSpec((CHUNK, D_MODEL), lambda b, c: (b * nc + c, COL_RWKV // D_MODEL + k))
    return pl.pallas_call(
        _rwkv_kernel,
        grid=(nb, nc),
        in_specs=[pl.BlockSpec((CHUNK, CHUNK), lambda b, c: (0, 0)), vec, lora, vec, lora, vec, vec, vec, vec, vec,
                  zblk(0), zblk(1), zblk(2), zblk(3),
                  pl.BlockSpec((CHUNK, RWKV_LORA), lambda b, c: (b * nc + c, 0))],
        out_specs=pl.BlockSpec((CHUNK, D_MODEL), lambda b, c: (b * nc + c, 0)),
        out_shape=jax.ShapeDtypeStruct((t, D_MODEL), BF16),
        scratch_shapes=[pltpu.VMEM((RWKV_HEADS, RWKV_N, RWKV_N), F32)],
        compiler_params=pltpu.CompilerParams(dimension_semantics=("parallel", "arbitrary"),
                                             vmem_limit_bytes=VMEM_LIMIT),
        name="rwkv7",
    )(tril, w0, wup, a0, aup, k_k, k_a, r_k, gn_w, gn_b, z, z, z, z, zl)


def _hgrn_plan():
    c = CHUNK
    p = np.zeros((HGRN_LEVELS + 1, c, c), np.float32)
    m = np.zeros((HGRN_LEVELS, c, c), np.float32)
    p[0] = np.tril(np.ones((c, c), np.float32))
    for lvl in range(HGRN_LEVELS):
        b = 1 << lvl
        for t in range(c):
            mid = (t // (2 * b)) * 2 * b + b
            if t >= mid:
                p[lvl + 1, t, mid:t + 1] = 1.0
                m[lvl, t, mid - b:mid] = 1.0
            else:
                p[lvl + 1, t, t + 1:mid] = 1.0
    return p.reshape(-1, c), m


def _hgrn_kernel(p_ref, m_ref, lbraw_ref, nw_ref, zq_ref, zf_ref, zi_ref, zg_ref, o_ref, st_ref, *, layer):
    c = pl.program_id(1)

    @pl.when(c == 0)
    def _():
        st_ref[...] = jnp.zeros_like(st_ref)

    lbraw = lbraw_ref[...]
    e = jnp.exp(lbraw - jnp.max(lbraw, axis=0, keepdims=True))
    sm = e / jnp.sum(e, axis=0, keepdims=True)
    lb = jnp.zeros((1, lbraw.shape[1]), F32)
    for i in range(1, layer + 1):
        lb = lb + sm[i:i + 1]
    f = lb + (1.0 - lb) * _sigmoid(zf_ref[...])
    g = jnp.log(f)
    x = _dot_exact_lhs(p_ref[...], g)
    lam_all = x[:CHUNK]
    e_lvl = jnp.exp(x[CHUNK:])
    q_all = _silu(zq_ref[...])
    k_all = 1.0 - f
    v_all = zi_ref[...]
    g_all = zg_ref[...]
    row = lax.broadcasted_iota(jnp.int32, (CHUNK, CHUNK), 0)
    col = lax.broadcasted_iota(jnp.int32, (CHUNK, CHUNK), 1)
    eye = row == col
    for h in range(HGRN_HEADS):
        sl = slice(h * HGRN_K, (h + 1) * HGRN_K)
        q, k, v, lam = q_all[:, sl], k_all[:, sl], v_all[:, sl], lam_all[:, sl]
        att = jnp.where(eye, jnp.sum(q * k, axis=-1, keepdims=True), 0.0)
        for lvl in range(HGRN_LEVELS):
            el = e_lvl[lvl * CHUNK:(lvl + 1) * CHUNK, sl]
            att = att + m_ref[lvl] * _dot_nt(q * el, k * el)
        st = st_ref[h]
        o = _dot(att, v) + _dot_nt(q * jnp.exp(lam), st)
        lam_end = lam[CHUNK - 1:CHUNK, :]
        st_ref[h] = st * jnp.exp(lam_end) + _dot_tn(v, k * jnp.exp(lam_end - lam))
        o = o * lax.rsqrt(jnp.mean(o * o, axis=-1, keepdims=True) + NORM_EPS) * nw_ref[...]
        o_ref[:, sl] = (o * _silu(g_all[:, sl])).astype(BF16)


def _hgrn(z, pmat, masks, lbraw, norm_w, *, lp, layer):
    t = z.shape[0]
    nb, nc = t // lp, lp // CHUNK
    depth = lbraw.shape[0]
    zblk = lambda k: pl.BlockSpec((CHUNK, D_MODEL), lambda b, c: (b * nc + c, COL_HGRN // D_MODEL + k))
    return pl.pallas_call(
        functools.partial(_hgrn_kernel, layer=layer),
        grid=(nb, nc),
        in_specs=[pl.BlockSpec(pmat.shape, lambda b, c: (0, 0)),
                  pl.BlockSpec(masks.shape, lambda b, c: (0, 0, 0)),
                  pl.BlockSpec((depth, D_MODEL), lambda b, c: (0, 0)),
                  pl.BlockSpec((1, HGRN_K), lambda b, c: (0, 0)),
                  zblk(0), zblk(1), zblk(2), zblk(3)],
        out_specs=pl.BlockSpec((CHUNK, D_MODEL), lambda b, c: (b * nc + c, 0)),
        out_shape=jax.ShapeDtypeStruct((t, D_MODEL), BF16),
        scratch_shapes=[pltpu.VMEM((HGRN_HEADS, HGRN_K, HGRN_K), F32)],
        compiler_params=pltpu.CompilerParams(dimension_semantics=("parallel", "arbitrary"),
                                             vmem_limit_bytes=VMEM_LIMIT),
        name="hgrn2",
    )(pmat, masks, lbraw, norm_w, z, z, z, z)


def _merge_kernel(h_ref, oa_ref, or_ref, oh_ref, ga_ref, gr_ref, gh_ref, wa_ref, wr_ref, wh_ref, wo_ref,
                  postw_ref, out_ref):
    d = functools.partial(jnp.dot, preferred_element_type=F32)
    y = (_sigmoid(ga_ref[...]) * d(oa_ref[...], wa_ref[...])
         + _sigmoid(gr_ref[...]) * d(or_ref[...], wr_ref[...])
         + _sigmoid(gh_ref[...]) * d(oh_ref[...], wh_ref[...]))
    y2 = d(y.astype(BF16), wo_ref[...])
    ms = jnp.mean(y2 * y2, axis=-1, keepdims=True)
    out_ref[...] = h_ref[...] + y2 * lax.rsqrt(ms + NORM_EPS) * postw_ref[...]


def _merge(h, o_att, o_rwkv, o_hgrn, z, w_att, w_rwkv, w_hgrn, w_o, post_w, *, tm):
    t = h.shape[0]
    rowblk = pl.BlockSpec((tm, D_MODEL), lambda i: (i, 0))
    gate = lambda k: pl.BlockSpec((tm, D_MODEL), lambda i: (i, COL_MG // D_MODEL + k))
    wblk = pl.BlockSpec((D_MODEL, D_MODEL), lambda i: (0, 0))
    return pl.pallas_call(
        _merge_kernel,
        grid=(t // tm,),
        in_specs=[rowblk, rowblk, rowblk, rowblk, gate(0), gate(1), gate(2), wblk, wblk, wblk, wblk,
                  pl.BlockSpec((1, D_MODEL), lambda i: (0, 0))],
        out_specs=rowblk,
        out_shape=jax.ShapeDtypeStruct((t, D_MODEL), F32),
        compiler_params=pltpu.CompilerParams(dimension_semantics=("parallel",),
                                             vmem_limit_bytes=VMEM_LIMIT),
        name="merge",
    )(h, o_att, o_rwkv, o_hgrn, z, z, z, w_att, w_rwkv, w_hgrn, w_o, post_w)


def _rotary_tables(lp):
    half = ATT_HEAD_DIM // 2
    inv = 1.0 / (ROPE_THETA ** (jnp.arange(0, ATT_HEAD_DIM, 2, dtype=F32) / ATT_HEAD_DIM))
    pos = jnp.maximum(jnp.arange(lp, dtype=F32) - PAD, 0.0)
    ang = pos[:, None] * inv[None, :]
    ang = jnp.concatenate([ang] * (LANES // half), axis=-1)
    return jnp.cos(ang), jnp.sin(ang)


def kernel(x, meta_tokens, pre_norm_w, post_norm_w, w_in, lambda_q1, lambda_k1, lambda_q2, lambda_k2, att_norm_w, rwkv_mu, rwkv_w0, rwkv_w_up, rwkv_a0, rwkv_a_up, rwkv_k_k, rwkv_k_a, rwkv_r_k, rwkv_gn_w, rwkv_gn_b, hgrn_lower_bounds, hgrn_norm_w, w_att_out, w_rwkv_out, w_hgrn_out, w_o):
    nb, seq, d = x.shape
    depth = w_in.shape[0]
    assert d == D_MODEL and seq % ROW_TILE == 0 and meta_tokens.shape == (N_META, D_MODEL)
    lp = seq + ROW_TILE
    t = nb * lp
    lora0 = COL_HGRN
    lora1 = lora0 + RWKV_LORA

    meta = jnp.broadcast_to(meta_tokens[None].astype(F32), (nb, N_META, D_MODEL))
    h = jnp.concatenate([jnp.zeros((nb, PAD, D_MODEL), F32), meta, x.astype(F32)], axis=1).reshape(t, D_MODEL)
    cos, sin = _rotary_tables(lp)
    tril = jnp.asarray(np.tril(np.ones((CHUNK, CHUNK), np.float32)), BF16)
    pmat_np, masks_np = _hgrn_plan()
    pmat, masks = jnp.asarray(pmat_np, BF16), jnp.asarray(masks_np, F32)
    row = lambda v: v.reshape(1, -1).astype(F32)

    for l in range(depth):
        w_main = jnp.concatenate([w_in[l, :, :lora0], w_in[l, :, lora1:]], axis=1).astype(BF16)
        wl_hi, wl_lo = _split(w_in[l, :, lora0:lora1].astype(F32))
        mu = rwkv_mu[l].astype(F32)
        mu_main = jnp.zeros((1, Z_W), F32).at[0, COL_RWKV:COL_HGRN].set(mu[:4 * D_MODEL])
        z, zl = _in_proj(h, row(pre_norm_w[l]), w_main, wl_hi, wl_lo, cos, sin, mu_main,
                         row(mu[4 * D_MODEL:]), lp=lp)
        lam_init = 0.8 - 0.6 * math.exp(-0.3 * l)
        o_att = _attention(z, row(lambda_q1[l]), row(lambda_k1[l]), row(lambda_q2[l]), row(lambda_k2[l]),
                           row(att_norm_w[l]), lp=lp, lam_init=lam_init)
        o_rwkv = _rwkv(z, zl, tril, row(rwkv_w0[l]), rwkv_w_up[l].astype(F32), row(rwkv_a0[l]),
                       rwkv_a_up[l].astype(F32), row(rwkv_k_k[l]), row(rwkv_k_a[l]), row(rwkv_r_k[l]),
                       row(rwkv_gn_w[l]), row(rwkv_gn_b[l]), lp=lp)
        o_hgrn = _hgrn(z, pmat, masks, hgrn_lower_bounds.astype(F32), row(hgrn_norm_w[l]), lp=lp, layer=l)
        h = _merge(h, o_att, o_rwkv, o_hgrn, z, w_att_out[l].astype(BF16), w_rwkv_out[l].astype(BF16),
                   w_hgrn_out[l].astype(BF16), w_o[l].astype(BF16), row(post_norm_w[l]), tm=lp // 4)
    return h.reshape(nb, lp, D_MODEL)[:, ROW_TILE:].astype(x.dtype)
```

```python
import functools
import math

import numpy as np
import jax
import jax.numpy as jnp
from jax import lax
from jax.experimental import pallas as pl
from jax.experimental.pallas import tpu as pltpu

F32 = jnp.float32
BF16 = jnp.bfloat16

D_MODEL = 1024
N_META = 16
LANES = 128
ROW_TILE = 128
PAD = ROW_TILE - N_META
ROPE_THETA = 10000.0
NORM_EPS = 1e-6
NEG_BIG = -1e30

ATT_HEAD_DIM = 64
ATT_HEADS = 8
ATT_TQ = 256
Q_SCALE = math.log2(math.e) * ATT_HEAD_DIM ** -0.5
RWKV_HEADS = 16
RWKV_N = 64
RWKV_GN_EPS = 64e-5
RWKV_LORA = 128
HGRN_HEADS = 8
HGRN_K = 128
CHUNK = 64
HGRN_LEVELS = 6

Z_W = 15 * D_MODEL
COL_ATT, COL_RWKV, COL_HGRN, COL_MG = 0, 4 * D_MODEL, 8 * D_MODEL, 12 * D_MODEL
IN_TN = 512
VMEM_LIMIT = 56 * 1024 * 1024


def _dot(a, b):
    return jnp.dot(a.astype(BF16), b.astype(BF16), preferred_element_type=F32)


def _dot_nt(a, b):
    return lax.dot_general(a.astype(BF16), b.astype(BF16), (((1,), (1,)), ((), ())),
                           preferred_element_type=F32)


def _dot_tn(a, b):
    return lax.dot_general(a.astype(BF16), b.astype(BF16), (((0,), (0,)), ((), ())),
                           preferred_element_type=F32)


def _split(x):
    hi = x.astype(BF16)
    lo = (x - hi.astype(F32)).astype(BF16)
    return hi, lo


def _dot_x3(a, b):
    ah, al = _split(a)
    bh, bl = _split(b)
    d = functools.partial(jnp.dot, preferred_element_type=F32)
    return d(ah, bh) + d(al, bh) + d(ah, bl)


def _dot_exact_lhs(p_bf16, x):
    xh, xl = _split(x)
    d = functools.partial(jnp.dot, preferred_element_type=F32)
    return d(p_bf16, xh) + d(p_bf16, xl)


def _sigmoid(x):
    return 1.0 / (1.0 + jnp.exp(-x))


def _silu(x):
    return x * _sigmoid(x)


def _inproj_kernel(h_ref, prew_ref, w_ref, wlh_ref, wll_ref, cos_ref, sin_ref, mu_ref, mul_ref,
                   z_ref, zl_ref, u_ref, *, tn):
    j = pl.program_id(1)
    rows = h_ref.shape[0]

    def shift_mix(acc, mu):
        prev = pltpu.roll(acc, 1, 0)
        row = lax.broadcasted_iota(jnp.int32, acc.shape, 0)
        prev = jnp.where(row == 0, 0.0, prev)
        return acc + (prev - acc) * mu

    @pl.when(j == 0)
    def _():
        x = h_ref[...]
        ms = jnp.mean(x * x, axis=-1, keepdims=True)
        u = x * lax.rsqrt(ms + NORM_EPS) * prew_ref[...]
        u_ref[...] = u.astype(BF16)
        uh, ul = _split(u)
        d = functools.partial(jnp.dot, preferred_element_type=F32)
        zl = d(uh, wlh_ref[...]) + d(ul, wlh_ref[...]) + d(uh, wll_ref[...])
        zl_ref[...] = shift_mix(zl, mul_ref[...])

    acc = jnp.dot(u_ref[...], w_ref[...], preferred_element_type=F32)
    n_q, n_qk = D_MODEL // tn, 2 * D_MODEL // tn
    is_rope = j < n_qk
    is_rwkv = (j >= COL_RWKV // tn) & (j < COL_HGRN // tn)

    @pl.when(is_rope)
    def _():
        scale = jnp.where(j < n_q, Q_SCALE, 1.0).astype(F32)
        lane = lax.broadcasted_iota(jnp.int32, (rows, LANES), 1)
        first_half = (lane % ATT_HEAD_DIM) < (ATT_HEAD_DIM // 2)
        cos = cos_ref[...] * scale
        sin = jnp.where(first_half, -sin_ref[...], sin_ref[...]) * scale
        for g in range(tn // LANES):
            x = acc[:, g * LANES:(g + 1) * LANES]
            rot = jnp.where(first_half, pltpu.roll(x, LANES - ATT_HEAD_DIM // 2, 1),
                            pltpu.roll(x, ATT_HEAD_DIM // 2, 1))
            z_ref[:, g * LANES:(g + 1) * LANES] = x * cos + rot * sin

    @pl.when(is_rwkv)
    def _():
        z_ref[...] = shift_mix(acc, mu_ref[...])

    @pl.when(jnp.logical_not(is_rope | is_rwkv))
    def _():
        z_ref[...] = acc


def _in_proj(h, pre_w, w_main, wl_hi, wl_lo, cos, sin, mu_main, mu_lora, *, lp):
    t = h.shape[0]
    nb = t // lp
    tn = IN_TN
    return pl.pallas_call(
        functools.partial(_inproj_kernel, tn=tn),
        grid=(nb, Z_W // tn),
        in_specs=[
            pl.BlockSpec((lp, D_MODEL), lambda i, j: (i, 0)),
            pl.BlockSpec((1, D_MODEL), lambda i, j: (0, 0)),
            pl.BlockSpec((D_MODEL, tn), lambda i, j: (0, j)),
            pl.BlockSpec((D_MODEL, RWKV_LORA), lambda i, j: (0, 0)),
            pl.BlockSpec((D_MODEL, RWKV_LORA), lambda i, j: (0, 0)),
            pl.BlockSpec((lp, LANES), lambda i, j: (0, 0)),
            pl.BlockSpec((lp, LANES), lambda i, j: (0, 0)),
            pl.BlockSpec((1, tn), lambda i, j: (0, j)),
            pl.BlockSpec((1, RWKV_LORA), lambda i, j: (0, 0)),
        ],
        out_specs=[
            pl.BlockSpec((lp, tn), lambda i, j: (i, j)),
            pl.BlockSpec((lp, RWKV_LORA), lambda i, j: (i, 0)),
        ],
        out_shape=[jax.ShapeDtypeStruct((t, Z_W), F32), jax.ShapeDtypeStruct((t, RWKV_LORA), F32)],
        scratch_shapes=[pltpu.VMEM((lp, D_MODEL), BF16)],
        compiler_params=pltpu.CompilerParams(dimension_semantics=("parallel", "arbitrary"),
                                             vmem_limit_bytes=VMEM_LIMIT),
        name="in_proj",
    )(h, pre_w, w_main, wl_hi, wl_lo, cos, sin, mu_main, mu_lora)


def _attn_kernel(lq1_ref, lk1_ref, lq2_ref, lk2_ref, nw_ref, q_ref, k_ref, v_ref, g_ref,
                 o_ref, kb_ref, vb_ref, *, lam_init):
    lp = q_ref.shape[0]
    lam = (jnp.exp(jnp.sum(lq1_ref[...] * lk1_ref[...], keepdims=True))
           - jnp.exp(jnp.sum(lq2_ref[...] * lk2_ref[...], keepdims=True)) + lam_init)
    kb_ref[...] = k_ref[...].astype(BF16)
    vb_ref[...] = v_ref[...].astype(BF16)
    out_scale = nw_ref[...] * (1.0 - lam_init)

    def tile(r0, tq):
        w = r0 + tq
        qb = q_ref[r0:w, :]
        half1 = lax.broadcasted_iota(jnp.int32, (tq, LANES), 1) < ATT_HEAD_DIM
        not_pad = lax.broadcasted_iota(jnp.int32, (tq, ROW_TILE), 1) >= PAD
        causal = (lax.broadcasted_iota(jnp.int32, (tq, tq), 1)
                  <= lax.broadcasted_iota(jnp.int32, (tq, tq), 0))
        if r0 == 0:
            bounds, masks = [(0, w)], [causal & not_pad]
        else:
            bounds = [(0, ROW_TILE), (ROW_TILE, r0), (r0, w)]
            masks = [not_pad, None, causal]
            if r0 == ROW_TILE:
                del bounds[1], masks[1]

        def softmax_pv(qh):
            parts = []
            for (c0, c1), mask in zip(bounds, masks):
                s = lax.dot_general(qh, kb_ref[c0:c1, :], (((1,), (1,)), ((), ())),
                                    preferred_element_type=F32)
                parts.append(s if mask is None else jnp.where(mask, s, NEG_BIG))
            m = functools.reduce(jnp.maximum, [jnp.max(s, axis=-1, keepdims=True) for s in parts])
            ps = [jnp.exp2(s - m) for s in parts]
            l = functools.reduce(jnp.add, [jnp.sum(p, axis=-1, keepdims=True) for p in ps])
            pv = functools.reduce(jnp.add, [
                jnp.dot(p.astype(BF16), vb_ref[c0:c1, :], preferred_element_type=F32)
                for p, (c0, c1) in zip(ps, bounds)])
            return pv * (1.0 / l)

        o = (softmax_pv(jnp.where(half1, qb, 0.0).astype(BF16))
             - lam * softmax_pv(jnp.where(half1, 0.0, qb).astype(BF16)))
        o = o * lax.rsqrt(jnp.mean(o * o, axis=-1, keepdims=True) + NORM_EPS) * out_scale
        o_ref[r0:w, :] = (o * _silu(g_ref[r0:w, :])).astype(BF16)

    tile(0, ROW_TILE)
    for r0 in range(ROW_TILE, lp, ATT_TQ):
        tile(r0, ATT_TQ)


def _attention(z, lq1, lk1, lq2, lk2, norm_w, *, lp, lam_init):
    t = z.shape[0]
    nb = t // lp
    small = lambda n: pl.BlockSpec((1, n), lambda b, h: (0, 0))
    colblk = lambda base: pl.BlockSpec((lp, LANES), lambda b, h: (b, base + h))
    return pl.pallas_call(
        functools.partial(_attn_kernel, lam_init=lam_init),
        grid=(nb, ATT_HEADS),
        in_specs=[small(ATT_HEAD_DIM)] * 4 + [small(LANES)]
                 + [colblk(0), colblk(ATT_HEADS), colblk(2 * ATT_HEADS), colblk(3 * ATT_HEADS)],
        out_specs=pl.BlockSpec((lp, LANES), lambda b, h: (b, h)),
        out_shape=jax.ShapeDtypeStruct((t, D_MODEL), BF16),
        scratch_shapes=[pltpu.VMEM((lp, LANES), BF16), pltpu.VMEM((lp, LANES), BF16)],
        compiler_params=pltpu.CompilerParams(dimension_semantics=("parallel", "parallel"),
                                             vmem_limit_bytes=VMEM_LIMIT),
        name="diff_attn",
    )(lq1, lk1, lq2, lk2, norm_w, z, z, z, z)


def _rwkv_kernel(tril_ref, w0_ref, wup_ref, a0_ref, aup_ref, kk_ref, ka_ref, rk_ref, gnw_ref, gnb_ref,
                 zr_ref, zk_ref, zv_ref, zg_ref, zl_ref, o_ref, ht_ref):
    c = pl.program_id(1)

    @pl.when(c == 0)
    def _():
        ht_ref[...] = jnp.zeros_like(ht_ref)

    n = RWKV_N
    zl = zl_ref[...]
    w_pre = w0_ref[...] + _dot_x3(jnp.tanh(zl[:, :n]), wup_ref[...])
    y = -w_pre
    softplus = jnp.maximum(y, 0.0) + jnp.log(1.0 + jnp.exp(-jnp.abs(y)))
    logw = -jnp.exp(-softplus - 0.5)
    a_all = _sigmoid(a0_ref[...] + _dot_x3(zl[:, n:], aup_ref[...]))
    lam_i_all = _dot_exact_lhs(tril_ref[...], logw)
    r_all, k_all, v_all, g_all = zr_ref[...], zk_ref[...], zv_ref[...], zg_ref[...]
    kk_all = k_all * kk_ref[...]
    k2_all = k_all * (1.0 + (a_all - 1.0) * ka_ref[...])
    rkr_all = r_all * k2_all * rk_ref[...]

    row = lax.broadcasted_iota(jnp.int32, (CHUNK, CHUNK), 0)
    col = lax.broadcasted_iota(jnp.int32, (CHUNK, CHUNK), 1)
    strict, incl = col < row, col <= row
    eye = jnp.where(row == col, 1.0, 0.0).astype(F32)

    heads = range(RWKV_HEADS)
    sls = [slice(h * n, (h + 1) * n) for h in heads]
    k2_, v_, kb_, lam_i_, rt_, at_, bt_, kt_ = ([] for _ in range(8))
    for sl in sls:
        kk = kk_all[:, sl]
        kk = kk / jnp.maximum(jnp.sqrt(jnp.sum(kk * kk, axis=-1, keepdims=True)), 1e-12)
        lam_i = lam_i_all[:, sl]
        e_ni = jnp.exp(-lam_i)
        kb = kk * a_all[:, sl]
        k2_.append(k2_all[:, sl])
        v_.append(v_all[:, sl])
        kb_.append(kb)
        lam_i_.append(lam_i)
        rt_.append(r_all[:, sl] * jnp.exp(lam_i))
        at_.append(-kk * jnp.exp(lam_i - logw[:, sl]))
        bt_.append(kb * e_ni)
        kt_.append(k2_all[:, sl] * e_ni)
    lhs_ = [jnp.concatenate([at_[h], rt_[h]], axis=0) for h in heads]
    mb_ = [_dot_nt(lhs_[h], bt_[h]) for h in heads]
    mk_ = [_dot_nt(lhs_[h], kt_[h]) for h in heads]
    a_ab_ = [jnp.where(strict, mb_[h][:CHUNK], 0.0) for h in heads]
    a_rb_ = [jnp.where(incl, mb_[h][CHUNK:], 0.0) for h in heads]
    av_ = [_dot(jnp.concatenate([jnp.where(strict, mk_[h][:CHUNK], 0.0),
                                 jnp.where(incl, mk_[h][CHUNK:], 0.0)], axis=0), v_[h]) for h in heads]
    tinv_ = [eye + a_ab_[h] for h in heads]
    p_ = a_ab_
    for _ in range(int(math.log2(CHUNK)) - 1):
        p_ = [_dot(p_[h], p_[h]) for h in heads]
        tinv_ = [tinv_[h] + _dot(tinv_[h], p_[h]) for h in heads]
    w_hat_ = [_dot(tinv_[h], at_[h]) for h in heads]
    u_hat_ = [_dot(tinv_[h], av_[h][:CHUNK]) for h in heads]
    ht_ = [ht_ref[h] for h in heads]
    g2_ = [_dot_nt(jnp.concatenate([w_hat_[h], rt_[h]], axis=0), ht_[h]) for h in heads]
    u_ = [g2_[h][:CHUNK] + u_hat_[h] for h in heads]
    o_ = [g2_[h][CHUNK:] + _dot(a_rb_[h], u_[h]) + av_[h][CHUNK:] for h in heads]
    for h in heads:
        lam_end = lam_i_[h][CHUNK - 1:CHUNK, :]
        e_end = jnp.exp(lam_end - lam_i_[h])
        ht_ref[h] = ht_[h] * jnp.exp(lam_end) + _dot_tn(
            jnp.concatenate([u_[h], v_[h]], axis=0),
            jnp.concatenate([kb_[h] * e_end, k2_[h] * e_end], axis=0))
    outs = []
    for h in heads:
        sl, o = sls[h], o_[h]
        mu = jnp.mean(o, axis=-1, keepdims=True)
        var = jnp.mean(jnp.square(o - mu), axis=-1, keepdims=True)
        o = (o - mu) * lax.rsqrt(var + RWKV_GN_EPS) * gnw_ref[:, sl] + gnb_ref[:, sl]
        o = o + jnp.sum(rkr_all[:, sl], axis=-1, keepdims=True) * v_[h]
        outs.append(o * _silu(g_all[:, sl]))
    o_ref[...] = jnp.concatenate(outs, axis=1).astype(BF16)


def _rwkv(z, zl, tril, w0, wup, a0, aup, k_k, k_a, r_k, gn_w, gn_b, *, lp):
    t = z.shape[0]
    nb, nc = t // lp, lp // CHUNK
    vec = pl.BlockSpec((1, D_MODEL), lambda b, c: (0, 0))
    lora = pl.BlockSpec((RWKV_N, D_MODEL), lambda b, c: (0, 0))
    zblk = lambda k: pl.BlockSpec((CHUNK, D_MODEL), lambda b, c: (b * nc + c, COL_RWKV // D_MODEL + k))
    return pl.pallas_call(
        _rwkv_kernel,
        grid=(nb, nc),
        in_specs=[pl.BlockSpec((CHUNK, CHUNK), lambda b, c: (0, 0)), vec, lora, vec, lora, vec, vec, vec, vec, vec,
                  zblk(0), zblk(1), zblk(2), zblk(3),
                  pl.BlockSpec((CHUNK, RWKV_LORA), lambda b, c: (b * nc + c, 0))],
        out_specs=pl.BlockSpec((CHUNK, D_MODEL), lambda b, c: (b * nc + c, 0)),
        out_shape=jax.ShapeDtypeStruct((t, D_MODEL), BF16),
        scratch_shapes=[pltpu.VMEM((RWKV_HEADS, RWKV_N, RWKV_N), F32)],
        compiler_params=pltpu.CompilerParams(dimension_semantics=("parallel", "arbitrary"),
                                             vmem_limit_bytes=VMEM_LIMIT),
        name="rwkv7",
    )(tril, w0, wup, a0, aup, k_k, k_a, r_k, gn_w, gn_b, z, z, z, z, zl)


def _hgrn_plan():
    c = CHUNK
    p = np.zeros((HGRN_LEVELS + 1, c, c), np.float32)
    m = np.zeros((HGRN_LEVELS, c, c), np.float32)
    p[0] = np.tril(np.ones((c, c), np.float32))
    for lvl in range(HGRN_LEVELS):
        b = 1 << lvl
        for t in range(c):
            mid = (t // (2 * b)) * 2 * b + b
            if t >= mid:
                p[lvl + 1, t, mid:t + 1] = 1.0
                m[lvl, t, mid - b:mid] = 1.0
            else:
                p[lvl + 1, t, t + 1:mid] = 1.0
    return p.reshape(-1, c), m


def _hgrn_kernel(p_ref, m_ref, lbraw_ref, nw_ref, zq_ref, zf_ref, zi_ref, zg_ref, o_ref, st_ref, *, layer):
    c = pl.program_id(1)

    @pl.when(c == 0)
    def _():
        st_ref[...] = jnp.zeros_like(st_ref)

    lbraw = lbraw_ref[...]
    e = jnp.exp(lbraw - jnp.max(lbraw, axis=0, keepdims=True))
    sm = e / jnp.sum(e, axis=0, keepdims=True)
    lb = jnp.zeros((1, lbraw.shape[1]), F32)
    for i in range(1, layer + 1):
        lb = lb + sm[i:i + 1]
    f = lb + (1.0 - lb) * _sigmoid(zf_ref[...])
    g = jnp.log(f)
    x = _dot_exact_lhs(p_ref[...], g)
    lam_all = x[:CHUNK]
    e_lvl = jnp.exp(x[CHUNK:])
    q_all = _silu(zq_ref[...])
    k_all = 1.0 - f
    v_all = zi_ref[...]
    g_all = zg_ref[...]
    row = lax.broadcasted_iota(jnp.int32, (CHUNK, CHUNK), 0)
    col = lax.broadcasted_iota(jnp.int32, (CHUNK, CHUNK), 1)
    eye = row == col
    heads = range(HGRN_HEADS)
    sls = [slice(h * HGRN_K, (h + 1) * HGRN_K) for h in heads]
    q_ = [q_all[:, sl] for sl in sls]
    k_ = [k_all[:, sl] for sl in sls]
    v_ = [v_all[:, sl] for sl in sls]
    lam_ = [lam_all[:, sl] for sl in sls]
    st_ = [st_ref[h] for h in heads]
    o_inter_ = [_dot_nt(q_[h] * jnp.exp(lam_[h]), st_[h]) for h in heads]
    att_ = [jnp.where(eye, jnp.sum(q_[h] * k_[h], axis=-1, keepdims=True), 0.0) for h in heads]
    for lvl in range(HGRN_LEVELS):
        el_ = [e_lvl[lvl * CHUNK:(lvl + 1) * CHUNK, sl] for sl in sls]
        att_ = [att_[h] + m_ref[lvl] * _dot_nt(q_[h] * el_[h], k_[h] * el_[h]) for h in heads]
    o_ = [_dot(att_[h], v_[h]) + o_inter_[h] for h in heads]
    for h in heads:
        lam_end = lam_[h][CHUNK - 1:CHUNK, :]
        st_ref[h] = st_[h] * jnp.exp(lam_end) + _dot_tn(v_[h], k_[h] * jnp.exp(lam_end - lam_[h]))
    for h in heads:
        o = o_[h]
        o = o * lax.rsqrt(jnp.mean(o * o, axis=-1, keepdims=True) + NORM_EPS) * nw_ref[...]
        o_ref[:, sls[h]] = (o * _silu(g_all[:, sls[h]])).astype(BF16)


def _hgrn(z, pmat, masks, lbraw, norm_w, *, lp, layer):
    t = z.shape[0]
    nb, nc = t // lp, lp // CHUNK
    depth = lbraw.shape[0]
    zblk = lambda k: pl.BlockSpec((CHUNK, D_MODEL), lambda b, c: (b * nc + c, COL_HGRN // D_MODEL + k))
    return pl.pallas_call(
        functools.partial(_hgrn_kernel, layer=layer),
        grid=(nb, nc),
        in_specs=[pl.BlockSpec(pmat.shape, lambda b, c: (0, 0)),
                  pl.BlockSpec(masks.shape, lambda b, c: (0, 0, 0)),
                  pl.BlockSpec((depth, D_MODEL), lambda b, c: (0, 0)),
                  pl.BlockSpec((1, HGRN_K), lambda b, c: (0, 0)),
                  zblk(0), zblk(1), zblk(2), zblk(3)],
        out_specs=pl.BlockSpec((CHUNK, D_MODEL), lambda b, c: (b * nc + c, 0)),
        out_shape=jax.ShapeDtypeStruct((t, D_MODEL), BF16),
        scratch_shapes=[pltpu.VMEM((HGRN_HEADS, HGRN_K, HGRN_K), F32)],
        compiler_params=pltpu.CompilerParams(dimension_semantics=("parallel", "arbitrary"),
                                             vmem_limit_bytes=VMEM_LIMIT),
        name="hgrn2",
    )(pmat, masks, lbraw, norm_w, z, z, z, z)


def _merge_kernel(h_ref, oa_ref, or_ref, oh_ref, ga_ref, gr_ref, gh_ref, wa_ref, wr_ref, wh_ref, wo_ref,
                  postw_ref, out_ref):
    d = functools.partial(jnp.dot, preferred_element_type=F32)
    y = (_sigmoid(ga_ref[...]) * d(oa_ref[...], wa_ref[...])
         + _sigmoid(gr_ref[...]) * d(or_ref[...], wr_ref[...])
         + _sigmoid(gh_ref[...]) * d(oh_ref[...], wh_ref[...]))
    y2 = d(y.astype(BF16), wo_ref[...])
    ms = jnp.mean(y2 * y2, axis=-1, keepdims=True)
    out_ref[...] = h_ref[...] + y2 * lax.rsqrt(ms + NORM_EPS) * postw_ref[...]


def _merge(h, o_att, o_rwkv, o_hgrn, z, w_att, w_rwkv, w_hgrn, w_o, post_w, *, tm):
    t = h.shape[0]
    rowblk = pl.BlockSpec((tm, D_MODEL), lambda i: (i, 0))
    gate = lambda k: pl.BlockSpec((tm, D_MODEL), lambda i: (i, COL_MG // D_MODEL + k))
    wblk = pl.BlockSpec((D_MODEL, D_MODEL), lambda i: (0, 0))
    return pl.pallas_call(
        _merge_kernel,
        grid=(t // tm,),
        in_specs=[rowblk, rowblk, rowblk, rowblk, gate(0), gate(1), gate(2), wblk, wblk, wblk, wblk,
                  pl.BlockSpec((1, D_MODEL), lambda i: (0, 0))],
        out_specs=rowblk,
        out_shape=jax.ShapeDtypeStruct((t, D_MODEL), F32),
        compiler_params=pltpu.CompilerParams(dimension_semantics=("parallel",),
                                             vmem_limit_bytes=VMEM_LIMIT),
        name="merge",
    )(h, o_att, o_rwkv, o_hgrn, z, z, z, w_att, w_rwkv, w_hgrn, w_o, post_w)


def _rotary_tables(lp):
    half = ATT_HEAD_DIM // 2
    inv = 1.0 / (ROPE_THETA ** (jnp.arange(0, ATT_HEAD_DIM, 2, dtype=F32) / ATT_HEAD_DIM))
    pos = jnp.maximum(jnp.arange(lp, dtype=F32) - PAD, 0.0)
    ang = pos[:, None] * inv[None, :]
    ang = jnp.concatenate([ang] * (LANES // half), axis=-1)
    return jnp.cos(ang), jnp.sin(ang)


def kernel(x, meta_tokens, pre_norm_w, post_norm_w, w_in, lambda_q1, lambda_k1, lambda_q2, lambda_k2, att_norm_w, rwkv_mu, rwkv_w0, rwkv_w_up, rwkv_a0, rwkv_a_up, rwkv_k_k, rwkv_k_a, rwkv_r_k, rwkv_gn_w, rwkv_gn_b, hgrn_lower_bounds, hgrn_norm_w, w_att_out, w_rwkv_out, w_hgrn_out, w_o):
    nb, seq, d = x.shape
    depth = w_in.shape[0]
    assert d == D_MODEL and seq % ATT_TQ == 0 and meta_tokens.shape == (N_META, D_MODEL)
    lp = seq + ROW_TILE
    t = nb * lp
    lora0 = COL_HGRN
    lora1 = lora0 + RWKV_LORA

    meta = jnp.broadcast_to(meta_tokens[None].astype(F32), (nb, N_META, D_MODEL))
    h = jnp.concatenate([jnp.zeros((nb, PAD, D_MODEL), F32), meta, x.astype(F32)], axis=1).reshape(t, D_MODEL)
    cos, sin = _rotary_tables(lp)
    tril = jnp.asarray(np.tril(np.ones((CHUNK, CHUNK), np.float32)), BF16)
    pmat_np, masks_np = _hgrn_plan()
    pmat, masks = jnp.asarray(pmat_np, BF16), jnp.asarray(masks_np, F32)
    row = lambda v: v.reshape(1, -1).astype(F32)

    for l in range(depth):
        w_main = jnp.concatenate([w_in[l, :, :lora0], w_in[l, :, lora1:]], axis=1).astype(BF16)
        wl_hi, wl_lo = _split(w_in[l, :, lora0:lora1].astype(F32))
        mu = rwkv_mu[l].astype(F32)
        mu_main = jnp.zeros((1, Z_W), F32).at[0, COL_RWKV:COL_HGRN].set(mu[:4 * D_MODEL])
        z, zl = _in_proj(h, row(pre_norm_w[l]), w_main, wl_hi, wl_lo, cos, sin, mu_main,
                         row(mu[4 * D_MODEL:]), lp=lp)
        lam_init = 0.8 - 0.6 * math.exp(-0.3 * l)
        o_att = _attention(z, row(lambda_q1[l]), row(lambda_k1[l]), row(lambda_q2[l]), row(lambda_k2[l]),
                           row(att_norm_w[l]), lp=lp, lam_init=lam_init)
        o_rwkv = _rwkv(z, zl, tril, row(rwkv_w0[l]), rwkv_w_up[l].astype(F32), row(rwkv_a0[l]),
                       rwkv_a_up[l].astype(F32), row(rwkv_k_k[l]), row(rwkv_k_a[l]), row(rwkv_r_k[l]),
                       row(rwkv_gn_w[l]), row(rwkv_gn_b[l]), lp=lp)
        o_hgrn = _hgrn(z, pmat, masks, hgrn_lower_bounds.astype(F32), row(hgrn_norm_w[l]), lp=lp, layer=l)
        h = _merge(h, o_att, o_rwkv, o_hgrn, z, w_att_out[l].astype(BF16), w_rwkv_out[l].astype(BF16),
                   w_hgrn_out[l].astype(BF16), w_o[l].astype(BF16), row(post_norm_w[l]), tm=lp // 4)
    return h.reshape(nb, lp, D_MODEL)[:, ROW_TILE:].astype(x.dtype)
```

```python
import functools
import math

import numpy as np
import jax
import jax.numpy as jnp
from jax import lax
from jax.experimental import pallas as pl
from jax.experimental.pallas import tpu as pltpu

F32 = jnp.float32
BF16 = jnp.bfloat16

D_MODEL = 1024
N_META = 16
LANES = 128
ROW_TILE = 128
PAD = ROW_TILE - N_META
ROPE_THETA = 10000.0
NORM_EPS = 1e-6
NEG_BIG = -1e30

ATT_HEAD_DIM = 64
ATT_HEADS = 8
ATT_TQ = 256
Q_SCALE = math.log2(math.e) * ATT_HEAD_DIM ** -0.5
RWKV_HEADS = 16
RWKV_N = 64
RWKV_GN_EPS = 64e-5
RWKV_LORA = 128
HGRN_HEADS = 8
HGRN_K = 128
CHUNK = 64
HGRN_LEVELS = 6

Z_W = 15 * D_MODEL
COL_ATT, COL_RWKV, COL_HGRN, COL_MG = 0, 4 * D_MODEL, 8 * D_MODEL, 12 * D_MODEL
IN_TN = 512
VMEM_LIMIT = 56 * 1024 * 1024


def _dot(a, b):
    return jnp.dot(a.astype(BF16), b.astype(BF16), preferred_element_type=F32)


def _dot_nt(a, b):
    return lax.dot_general(a.astype(BF16), b.astype(BF16), (((1,), (1,)), ((), ())),
                           preferred_element_type=F32)


def _dot_tn(a, b):
    return lax.dot_general(a.astype(BF16), b.astype(BF16), (((0,), (0,)), ((), ())),
                           preferred_element_type=F32)


def _split(x):
    hi = x.astype(BF16)
    lo = (x - hi.astype(F32)).astype(BF16)
    return hi, lo


def _dot_x3(a, b):
    ah, al = _split(a)
    bh, bl = _split(b)
    d = functools.partial(jnp.dot, preferred_element_type=F32)
    return d(ah, bh) + d(al, bh) + d(ah, bl)


def _dot_exact_lhs(p_bf16, x):
    xh, xl = _split(x)
    d = functools.partial(jnp.dot, preferred_element_type=F32)
    return d(p_bf16, xh) + d(p_bf16, xl)


def _sigmoid(x):
    return 1.0 / (1.0 + jnp.exp(-x))


def _silu(x):
    return x * _sigmoid(x)


def _inproj_kernel(h_ref, prew_ref, w_ref, wlh_ref, wll_ref, cos_ref, sin_ref, mu_ref, mul_ref,
                   z_ref, zl_ref, u_ref, *, tn):
    j = pl.program_id(1)
    rows = h_ref.shape[0]

    def shift_mix(acc, mu):
        prev = pltpu.roll(acc, 1, 0)
        row = lax.broadcasted_iota(jnp.int32, acc.shape, 0)
        prev = jnp.where(row == 0, 0.0, prev)
        return acc + (prev - acc) * mu

    @pl.when(j == 0)
    def _():
        x = h_ref[...]
        ms = jnp.mean(x * x, axis=-1, keepdims=True)
        u = x * lax.rsqrt(ms + NORM_EPS) * prew_ref[...]
        u_ref[...] = u.astype(BF16)
        uh, ul = _split(u)
        d = functools.partial(jnp.dot, preferred_element_type=F32)
        zl = d(uh, wlh_ref[...]) + d(ul, wlh_ref[...]) + d(uh, wll_ref[...])
        zl_ref[...] = shift_mix(zl, mul_ref[...])

    acc = jnp.dot(u_ref[...], w_ref[...], preferred_element_type=F32)
    n_q, n_qk = D_MODEL // tn, 2 * D_MODEL // tn
    is_rope = j < n_qk
    is_rwkv = (j >= COL_RWKV // tn) & (j < COL_HGRN // tn)

    @pl.when(is_rope)
    def _():
        scale = jnp.where(j < n_q, Q_SCALE, 1.0).astype(F32)
        lane = lax.broadcasted_iota(jnp.int32, (rows, LANES), 1)
        first_half = (lane % ATT_HEAD_DIM) < (ATT_HEAD_DIM // 2)
        cos = cos_ref[...] * scale
        sin = jnp.where(first_half, -sin_ref[...], sin_ref[...]) * scale
        for g in range(tn // LANES):
            x = acc[:, g * LANES:(g + 1) * LANES]
            rot = jnp.where(first_half, pltpu.roll(x, LANES - ATT_HEAD_DIM // 2, 1),
                            pltpu.roll(x, ATT_HEAD_DIM // 2, 1))
            z_ref[:, g * LANES:(g + 1) * LANES] = (x * cos + rot * sin).astype(BF16)

    @pl.when(is_rwkv)
    def _():
        z_ref[...] = shift_mix(acc, mu_ref[...]).astype(BF16)

    @pl.when(jnp.logical_not(is_rope | is_rwkv))
    def _():
        z_ref[...] = acc.astype(BF16)


def _in_proj(h, pre_w, w_main, wl_hi, wl_lo, cos, sin, mu_main, mu_lora, *, lp):
    t = h.shape[0]
    nb = t // lp
    tn = IN_TN
    return pl.pallas_call(
        functools.partial(_inproj_kernel, tn=tn),
        grid=(nb, Z_W // tn),
        in_specs=[
            pl.BlockSpec((lp, D_MODEL), lambda i, j: (i, 0)),
            pl.BlockSpec((1, D_MODEL), lambda i, j: (0, 0)),
            pl.BlockSpec((D_MODEL, tn), lambda i, j: (0, j)),
            pl.BlockSpec((D_MODEL, RWKV_LORA), lambda i, j: (0, 0)),
            pl.BlockSpec((D_MODEL, RWKV_LORA), lambda i, j: (0, 0)),
            pl.BlockSpec((lp, LANES), lambda i, j: (0, 0)),
            pl.BlockSpec((lp, LANES), lambda i, j: (0, 0)),
            pl.BlockSpec((1, tn), lambda i, j: (0, j)),
            pl.BlockSpec((1, RWKV_LORA), lambda i, j: (0, 0)),
        ],
        out_specs=[
            pl.BlockSpec((lp, tn), lambda i, j: (i, j)),
            pl.BlockSpec((lp, RWKV_LORA), lambda i, j: (i, 0)),
        ],
        out_shape=[jax.ShapeDtypeStruct((t, Z_W), BF16), jax.ShapeDtypeStruct((t, RWKV_LORA), F32)],
        scratch_shapes=[pltpu.VMEM((lp, D_MODEL), BF16)],
        compiler_params=pltpu.CompilerParams(dimension_semantics=("parallel", "arbitrary"),
                                             vmem_limit_bytes=VMEM_LIMIT),
        name="in_proj",
    )(h, pre_w, w_main, wl_hi, wl_lo, cos, sin, mu_main, mu_lora)


def _attn_kernel(lq1_ref, lk1_ref, lq2_ref, lk2_ref, nw_ref, q_ref, k_ref, v_ref, g_ref,
                 o_ref, vx_ref, p_ref, *, lam_init):
    lp = q_ref.shape[0]
    lam = (jnp.exp(jnp.sum(lq1_ref[...] * lk1_ref[...], keepdims=True))
           - jnp.exp(jnp.sum(lq2_ref[...] * lk2_ref[...], keepdims=True)) + lam_init)
    vx_ref[:, :LANES] = v_ref[...]
    vx_ref[:, LANES:] = jnp.ones((lp, LANES), BF16)
    out_scale = nw_ref[...] * (1.0 - lam_init)

    def layout(r0, tq):
        not_pad = lax.broadcasted_iota(jnp.int32, (tq, ROW_TILE), 1) >= PAD
        causal = (lax.broadcasted_iota(jnp.int32, (tq, tq), 1)
                  <= lax.broadcasted_iota(jnp.int32, (tq, tq), 0))
        if r0 == 0:
            return [(0, tq)], [causal & not_pad]
        bounds = [(0, ROW_TILE), (ROW_TILE, r0), (r0, r0 + tq)]
        masks = [not_pad, None, causal]
        if r0 == ROW_TILE:
            del bounds[1], masks[1]
        return bounds, masks

    def scores(r0, tq, half):
        bounds, masks = layout(r0, tq)
        qb = q_ref[r0:r0 + tq, :].astype(F32)
        half1 = lax.broadcasted_iota(jnp.int32, (tq, LANES), 1) < ATT_HEAD_DIM
        qh = (jnp.where(half1, qb, 0.0) if half == 0 else jnp.where(half1, 0.0, qb)).astype(BF16)
        parts = []
        for (c0, c1), mask in zip(bounds, masks):
            s = lax.dot_general(qh, k_ref[c0:c1, :], (((1,), (1,)), ((), ())), preferred_element_type=F32)
            parts.append(s if mask is None else jnp.where(mask, s, NEG_BIG))
        return parts

    def probs(r0, tq, parts, slot):
        bounds, _ = layout(r0, tq)
        cols = [s[:, c:c + LANES] for s in parts for c in range(0, s.shape[1], LANES)]
        m = jnp.max(functools.reduce(jnp.maximum, cols), axis=-1, keepdims=True)
        for s, (c0, c1) in zip(parts, bounds):
            p_ref[slot, 0:tq, c0:c1] = jnp.exp2((s - m).astype(BF16))

    def weighted(r0, tq, slot):
        w = r0 + tq
        pvl = jnp.dot(p_ref[slot, 0:tq, 0:w], vx_ref[0:w, :], preferred_element_type=F32)
        return pvl[:, :LANES] * (1.0 / pvl[:, LANES:])

    def finish(r0, tq, o1, o2):
        o = o1 - lam * o2
        o = o * lax.rsqrt(jnp.mean(o * o, axis=-1, keepdims=True) + NORM_EPS) * out_scale
        o_ref[r0:r0 + tq, :] = (o * _silu(g_ref[r0:r0 + tq, :].astype(F32))).astype(BF16)

    tiles = [(0, ROW_TILE)] + [(r0, ATT_TQ) for r0 in range(ROW_TILE, lp, ATT_TQ)]
    units = [(r0, tq, half) for r0, tq in tiles for half in (0, 1)]
    parts_of, out_of = {}, {}
    for k in range(len(units) + 2):
        if k < len(units):
            parts_of[k] = scores(*units[k])
        if 0 <= k - 1 < len(units):
            r0, tq, _ = units[k - 1]
            probs(r0, tq, parts_of.pop(k - 1), (k - 1) % 2)
        if 0 <= k - 2 < len(units):
            r0, tq, half = units[k - 2]
            out_of[half] = weighted(r0, tq, (k - 2) % 2)
            if half == 1:
                finish(r0, tq, out_of[0], out_of[1])


def _attention(z, lq1, lk1, lq2, lk2, norm_w, *, lp, lam_init):
    t = z.shape[0]
    nb = t // lp
    small = lambda n: pl.BlockSpec((1, n), lambda b, h: (0, 0))
    colblk = lambda base: pl.BlockSpec((lp, LANES), lambda b, h: (b, base + h))
    return pl.pallas_call(
        functools.partial(_attn_kernel, lam_init=lam_init),
        grid=(nb, ATT_HEADS),
        in_specs=[small(ATT_HEAD_DIM)] * 4 + [small(LANES)]
                 + [colblk(0), colblk(ATT_HEADS), colblk(2 * ATT_HEADS), colblk(3 * ATT_HEADS)],
        out_specs=pl.BlockSpec((lp, LANES), lambda b, h: (b, h)),
        out_shape=jax.ShapeDtypeStruct((t, D_MODEL), BF16),
        scratch_shapes=[pltpu.VMEM((lp, 2 * LANES), BF16), pltpu.VMEM((2, ATT_TQ, lp), BF16)],
        compiler_params=pltpu.CompilerParams(dimension_semantics=("parallel", "parallel"),
                                             vmem_limit_bytes=VMEM_LIMIT),
        name="diff_attn",
    )(lq1, lk1, lq2, lk2, norm_w, z, z, z, z)


def _rwkv_kernel(tril_ref, ones_ref, w0_ref, wuh_ref, wul_ref, a0_ref, auh_ref, aul_ref, kk_ref, ka_ref, rk_ref,
                 gnw_ref, gnb_ref, zr_ref, zk_ref, zv_ref, zg_ref, zl_ref, o_ref, ht_ref):
    c = pl.program_id(1)

    @pl.when(c == 0)
    def _():
        ht_ref[...] = jnp.zeros_like(ht_ref)

    n = RWKV_N
    d = functools.partial(jnp.dot, preferred_element_type=F32)

    def lora(x, wh_ref, wl_ref):
        xh, xl = _split(x)
        return d(xh, wh_ref[...]) + d(xl, wh_ref[...]) + d(xh, wl_ref[...])

    def head_sum(x):
        return jnp.concatenate([d(x[:, p * LANES:(p + 1) * LANES].astype(BF16), ones_ref[...])
                                for p in range(x.shape[1] // LANES)], axis=1)

    zl = zl_ref[...]
    y = -(w0_ref[...] + lora(jnp.tanh(zl[:, :n]), wuh_ref, wul_ref))
    softplus = jnp.maximum(y, 0.0) + jnp.log(1.0 + jnp.exp(-jnp.abs(y)))
    logw = -jnp.exp(-softplus - 0.5)
    a_all = _sigmoid(a0_ref[...] + lora(zl[:, n:], auh_ref, aul_ref))
    lam_i = _dot_exact_lhs(tril_ref[...], logw)
    r_all, k_all = zr_ref[...].astype(F32), zk_ref[...].astype(F32)
    v_all, g_all = zv_ref[...].astype(F32), zg_ref[...].astype(F32)
    kk = k_all * kk_ref[...]
    kk = kk * lax.rsqrt(jnp.maximum(head_sum(kk * kk), 1e-24))
    k2 = k_all * (1.0 + (a_all - 1.0) * ka_ref[...])
    kb = kk * a_all
    e_ni = jnp.exp(-lam_i)
    rt = r_all * jnp.exp(lam_i)
    at = -kk * jnp.exp(lam_i - logw)
    bt = kb * e_ni
    kt = k2 * e_ni
    lam_end = lam_i[CHUNK - 1:CHUNK, :]
    e_end = jnp.exp(lam_end - lam_i)
    kbe = kb * e_end
    k2e = k2 * e_end
    dec_end = jnp.exp(lam_end)
    bonus = head_sum(r_all * k2 * rk_ref[...]) * v_all

    lane = lax.broadcasted_iota(jnp.int32, (CHUNK, LANES), 1)
    t_idx = lax.broadcasted_iota(jnp.int32, (CHUNK, LANES), 0)
    lo = lane < n
    s_idx = lane % n
    eye = jnp.where(s_idx == t_idx, 1.0, 0.0).astype(F32)
    t2 = lax.broadcasted_iota(jnp.int32, (2 * CHUNK, LANES), 0)
    s2 = lax.broadcasted_iota(jnp.int32, (2 * CHUNK, LANES), 1) % n
    tri2 = s2 < jnp.where(t2 < CHUNK, t2, t2 - CHUNK + 1)
    blk =((lax.broadcasted_iota(jnp.int32, (LANES, LANES), 0) < n)
           == (lax.broadcasted_iota(jnp.int32, (LANES, LANES), 1) < n))

    def bd(x):
        return jnp.concatenate([jnp.where(lo, x, 0.0), jnp.where(lo, 0.0, x)], axis=0).astype(BF16)

    def mm(a, b):
        return d(a.astype(BF16), b)

    pairs = range(RWKV_HEADS // 2)
    sls = [slice(p * LANES, (p + 1) * LANES) for p in pairs]
    lhs_ = [jnp.concatenate([at[:, sl], rt[:, sl]], axis=0) for sl in sls]
    mb_ = [jnp.where(tri2, _dot_nt(lhs_[p], bd(bt[:, sls[p]])), 0.0) for p in pairs]
    mk_ = [jnp.where(tri2, _dot_nt(lhs_[p], bd(kt[:, sls[p]])), 0.0) for p in pairs]
    av_ = [mm(mk_[p], bd(v_all[:, sls[p]])) for p in pairs]
    p_ = [mb_[p][:CHUNK] for p in pairs]
    tinv_ = [eye + p_[p] for p in pairs]
    for _ in range(int(math.log2(CHUNK)) - 1):
        p_ = [mm(p_[p], bd(p_[p])) for p in pairs]
        tinv_ = [tinv_[p] + mm(tinv_[p], bd(p_[p])) for p in pairs]
    w_hat_ = [mm(tinv_[p], bd(at[:, sls[p]])) for p in pairs]
    u_hat_ = [mm(tinv_[p], bd(av_[p][:CHUNK])) for p in pairs]
    ht_ = [ht_ref[p] for p in pairs]
    g2_ = [_dot_nt(jnp.concatenate([w_hat_[p], rt[:, sls[p]]], axis=0), ht_[p]) for p in pairs]
    u_ = [g2_[p][:CHUNK] + u_hat_[p] for p in pairs]
    o_ = [g2_[p][CHUNK:] + mm(mb_[p][CHUNK:], bd(u_[p])) + av_[p][CHUNK:] for p in pairs]
    for p in pairs:
        upd = _dot_tn(jnp.concatenate([u_[p], v_all[:, sls[p]]], axis=0),
                      jnp.concatenate([kbe[:, sls[p]], k2e[:, sls[p]]], axis=0))
        ht_ref[p] = jnp.where(blk, ht_[p] * dec_end[:, sls[p]] + upd, 0.0)
    o = jnp.concatenate(o_, axis=1)
    cen = o - head_sum(o) * (1.0 / n)
    var = head_sum(cen * cen) * (1.0 / n)
    o = cen * lax.rsqrt(var + RWKV_GN_EPS) * gnw_ref[...] + gnb_ref[...] + bonus
    o_ref[...] = (o * _silu(g_all)).astype(BF16)


def _rwkv(z, zl, tril, ones_bd, w0, wup, a0, aup, k_k, k_a, r_k, gn_w, gn_b, *, lp):
    t = z.shape[0]
    nb, nc = t // lp, lp // CHUNK
    const = lambda a: pl.BlockSpec(a.shape, lambda b, c: (0,) * a.ndim)
    zblk = lambda k: pl.BlockSpec((CHUNK, D_MODEL), lambda b, c: (b * nc + c, COL_RWKV // D_MODEL + k))
    params = [tril, ones_bd, w0, *_split(wup), a0, *_split(aup), k_k, k_a, r_k, gn_w, gn_b]
    return pl.pallas_call(
        _rwkv_kernel,
        grid=(nb, nc),
        in_specs=[const(a) for a in params] + [zblk(0), zblk(1), zblk(2), zblk(3),
                  pl.BlockSpec((CHUNK, RWKV_LORA), lambda b, c: (b * nc + c, 0))],
        out_specs=pl.BlockSpec((CHUNK, D_MODEL), lambda b, c: (b * nc + c, 0)),
        out_shape=jax.ShapeDtypeStruct((t, D_MODEL), BF16),
        scratch_shapes=[pltpu.VMEM((RWKV_HEADS // 2, LANES, LANES), F32)],
        compiler_params=pltpu.CompilerParams(dimension_semantics=("parallel", "arbitrary"),
                                             vmem_limit_bytes=VMEM_LIMIT),
        name="rwkv7",
    )(*params, z, z, z, z, zl)


def _hgrn_plan():
    c = CHUNK
    p = np.zeros((HGRN_LEVELS + 1, c, c), np.float32)
    m = np.zeros((HGRN_LEVELS, c, c), np.float32)
    p[0] = np.tril(np.ones((c, c), np.float32))
    for lvl in range(HGRN_LEVELS):
        b = 1 << lvl
        for t in range(c):
            mid = (t // (2 * b)) * 2 * b + b
            if t >= mid:
                p[lvl + 1, t, mid:t + 1] = 1.0
                m[lvl, t, mid - b:mid] = 1.0
            else:
                p[lvl + 1, t, t + 1:mid] = 1.0
    return p.reshape(-1, c), m


def _hgrn_kernel(p_ref, m_ref, lbraw_ref, nw_ref, zq_ref, zf_ref, zi_ref, zg_ref, o_ref, st_ref, *, layer):
    c = pl.program_id(1)

    @pl.when(c == 0)
    def _():
        st_ref[...] = jnp.zeros_like(st_ref)

    lbraw = lbraw_ref[...]
    e = jnp.exp(lbraw - jnp.max(lbraw, axis=0, keepdims=True))
    sm = e / jnp.sum(e, axis=0, keepdims=True)
    lb = jnp.zeros((1, lbraw.shape[1]), F32)
    for i in range(1, layer + 1):
        lb = lb + sm[i:i + 1]
    f = lb + (1.0 - lb) * _sigmoid(zf_ref[...].astype(F32))
    g = jnp.log(f)
    x = _dot_exact_lhs(p_ref[...], g)
    lam_all = x[:CHUNK]
    e_lvl = jnp.exp(x[CHUNK:])
    q_all = _silu(zq_ref[...].astype(F32))
    k_all = 1.0 - f
    v_all = zi_ref[...]
    g_all = zg_ref[...].astype(F32)
    row = lax.broadcasted_iota(jnp.int32, (CHUNK, CHUNK), 0)
    col = lax.broadcasted_iota(jnp.int32, (CHUNK, CHUNK), 1)
    eye = row == col
    heads = range(HGRN_HEADS)
    sls = [slice(h * HGRN_K, (h + 1) * HGRN_K) for h in heads]
    q_ = [q_all[:, sl] for sl in sls]
    k_ = [k_all[:, sl] for sl in sls]
    v_ = [v_all[:, sl] for sl in sls]
    lam_ = [lam_all[:, sl] for sl in sls]
    st_ = [st_ref[h] for h in heads]
    o_inter_ = [_dot_nt(q_[h] * jnp.exp(lam_[h]), st_[h]) for h in heads]
    att_ = [jnp.where(eye, jnp.sum(q_[h] * k_[h], axis=-1, keepdims=True), 0.0) for h in heads]
    for lvl in range(HGRN_LEVELS):
        el_ = [e_lvl[lvl * CHUNK:(lvl + 1) * CHUNK, sl] for sl in sls]
        att_ = [att_[h] + m_ref[lvl] * _dot_nt(q_[h] * el_[h], k_[h] * el_[h]) for h in heads]
    o_ = [_dot(att_[h], v_[h]) + o_inter_[h] for h in heads]
    for h in heads:
        lam_end = lam_[h][CHUNK - 1:CHUNK, :]
        st_ref[h] = st_[h] * jnp.exp(lam_end) + _dot_tn(v_[h], k_[h] * jnp.exp(lam_end - lam_[h]))
    for h in heads:
        o = o_[h]
        o = o * lax.rsqrt(jnp.mean(o * o, axis=-1, keepdims=True) + NORM_EPS) * nw_ref[...]
        o_ref[:, sls[h]] = (o * _silu(g_all[:, sls[h]])).astype(BF16)


def _hgrn(z, pmat, masks, lbraw, norm_w, *, lp, layer):
    t = z.shape[0]
    nb, nc = t // lp, lp // CHUNK
    depth = lbraw.shape[0]
    zblk = lambda k: pl.BlockSpec((CHUNK, D_MODEL), lambda b, c: (b * nc + c, COL_HGRN // D_MODEL + k))
    return pl.pallas_call(
        functools.partial(_hgrn_kernel, layer=layer),
        grid=(nb, nc),
        in_specs=[pl.BlockSpec(pmat.shape, lambda b, c: (0, 0)),
                  pl.BlockSpec(masks.shape, lambda b, c: (0, 0, 0)),
                  pl.BlockSpec((depth, D_MODEL), lambda b, c: (0, 0)),
                  pl.BlockSpec((1, HGRN_K), lambda b, c: (0, 0)),
                  zblk(0), zblk(1), zblk(2), zblk(3)],
        out_specs=pl.BlockSpec((CHUNK, D_MODEL), lambda b, c: (b * nc + c, 0)),
        out_shape=jax.ShapeDtypeStruct((t, D_MODEL), BF16),
        scratch_shapes=[pltpu.VMEM((HGRN_HEADS, HGRN_K, HGRN_K), F32)],
        compiler_params=pltpu.CompilerParams(dimension_semantics=("parallel", "arbitrary"),
                                             vmem_limit_bytes=VMEM_LIMIT),
        name="hgrn2",
    )(pmat, masks, lbraw, norm_w, z, z, z, z)


def _merge_kernel(h_ref, oa_ref, or_ref, oh_ref, ga_ref, gr_ref, gh_ref, wa_ref, wr_ref, wh_ref, wo_ref,
                  postw_ref, out_ref):
    d = functools.partial(jnp.dot, preferred_element_type=F32)
    y = (_sigmoid(ga_ref[...].astype(F32)) * d(oa_ref[...], wa_ref[...])
         + _sigmoid(gr_ref[...].astype(F32)) * d(or_ref[...], wr_ref[...])
         + _sigmoid(gh_ref[...].astype(F32)) * d(oh_ref[...], wh_ref[...]))
    y2 = d(y.astype(BF16), wo_ref[...])
    ms = jnp.mean(y2 * y2, axis=-1, keepdims=True)
    out_ref[...] = h_ref[...] + y2 * lax.rsqrt(ms + NORM_EPS) * postw_ref[...]


def _merge(h, o_att, o_rwkv, o_hgrn, z, w_att, w_rwkv, w_hgrn, w_o, post_w, *, tm):
    t = h.shape[0]
    rowblk = pl.BlockSpec((tm, D_MODEL), lambda i: (i, 0))
    gate = lambda k: pl.BlockSpec((tm, D_MODEL), lambda i: (i, COL_MG // D_MODEL + k))
    wblk = pl.BlockSpec((D_MODEL, D_MODEL), lambda i: (0, 0))
    return pl.pallas_call(
        _merge_kernel,
        grid=(t // tm,),
        in_specs=[rowblk, rowblk, rowblk, rowblk, gate(0), gate(1), gate(2), wblk, wblk, wblk, wblk,
                  pl.BlockSpec((1, D_MODEL), lambda i: (0, 0))],
        out_specs=rowblk,
        out_shape=jax.ShapeDtypeStruct((t, D_MODEL), F32),
        compiler_params=pltpu.CompilerParams(dimension_semantics=("parallel",),
                                             vmem_limit_bytes=VMEM_LIMIT),
        name="merge",
    )(h, o_att, o_rwkv, o_hgrn, z, z, z, w_att, w_rwkv, w_hgrn, w_o, post_w)


def _rotary_tables(lp):
    half = ATT_HEAD_DIM // 2
    inv = 1.0 / (ROPE_THETA ** (jnp.arange(0, ATT_HEAD_DIM, 2, dtype=F32) / ATT_HEAD_DIM))
    pos = jnp.maximum(jnp.arange(lp, dtype=F32) - PAD, 0.0)
    ang = pos[:, None] * inv[None, :]
    ang = jnp.concatenate([ang] * (LANES // half), axis=-1)
    return jnp.cos(ang), jnp.sin(ang)


def kernel(x, meta_tokens, pre_norm_w, post_norm_w, w_in, lambda_q1, lambda_k1, lambda_q2, lambda_k2, att_norm_w, rwkv_mu, rwkv_w0, rwkv_w_up, rwkv_a0, rwkv_a_up, rwkv_k_k, rwkv_k_a, rwkv_r_k, rwkv_gn_w, rwkv_gn_b, hgrn_lower_bounds, hgrn_norm_w, w_att_out, w_rwkv_out, w_hgrn_out, w_o):
    nb, seq, d = x.shape
    depth = w_in.shape[0]
    assert d == D_MODEL and seq % ATT_TQ == 0 and meta_tokens.shape == (N_META, D_MODEL)
    lp = seq + ROW_TILE
    t = nb * lp
    lora0 = COL_HGRN
    lora1 = lora0 + RWKV_LORA

    meta = jnp.broadcast_to(meta_tokens[None].astype(F32), (nb, N_META, D_MODEL))
    h = jnp.concatenate([jnp.zeros((nb, PAD, D_MODEL), F32), meta, x.astype(F32)], axis=1).reshape(t, D_MODEL)
    cos, sin = _rotary_tables(lp)
    tril = jnp.asarray(np.tril(np.ones((CHUNK, CHUNK), np.float32)), BF16)
    head_of_lane = np.arange(LANES) // RWKV_N
    ones_bd = jnp.asarray(head_of_lane[:, None] == head_of_lane[None, :], BF16)
    pmat_np, masks_np = _hgrn_plan()
    pmat, masks = jnp.asarray(pmat_np, BF16), jnp.asarray(masks_np, F32)
    row = lambda v: v.reshape(1, -1).astype(F32)

    for l in range(depth):
        w_main = jnp.concatenate([w_in[l, :, :lora0], w_in[l, :, lora1:]], axis=1).astype(BF16)
        wl_hi, wl_lo = _split(w_in[l, :, lora0:lora1].astype(F32))
        mu = rwkv_mu[l].astype(F32)
        mu_main = jnp.zeros((1, Z_W), F32).at[0, COL_RWKV:COL_HGRN].set(mu[:4 * D_MODEL])
        z, zl = _in_proj(h, row(pre_norm_w[l]), w_main, wl_hi, wl_lo, cos, sin, mu_main,
                         row(mu[4 * D_MODEL:]), lp=lp)
        lam_init = 0.8 - 0.6 * math.exp(-0.3 * l)
        o_att = _attention(z, row(lambda_q1[l]), row(lambda_k1[l]), row(lambda_q2[l]), row(lambda_k2[l]),
                           row(att_norm_w[l]), lp=lp, lam_init=lam_init)
        o_rwkv = _rwkv(z, zl, tril, ones_bd, row(rwkv_w0[l]), rwkv_w_up[l].astype(F32), row(rwkv_a0[l]),
                       rwkv_a_up[l].astype(F32), row(rwkv_k_k[l]), row(rwkv_k_a[l]), row(rwkv_r_k[l]),
                       row(rwkv_gn_w[l]), row(rwkv_gn_b[l]), lp=lp)
        o_hgrn = _hgrn(z, pmat, masks, hgrn_lower_bounds.astype(F32), row(hgrn_norm_w[l]), lp=lp, layer=l)
        h = _merge(h, o_att, o_rwkv, o_hgrn, z, w_att_out[l].astype(BF16), w_rwkv_out[l].astype(BF16),
                   w_hgrn_out[l].astype(BF16), w_o[l].astype(BF16), row(post_norm_w[l]), tm=lp // 4)
    return h.reshape(nb, lp, D_MODEL)[:, ROW_TILE:].astype(x.dtype)
```

```python
import functools
import math

import numpy as np
import jax
import jax.numpy as jnp
from jax import lax
from jax.experimental import pallas as pl
from jax.experimental.pallas import tpu as pltpu

F32 = jnp.float32
BF16 = jnp.bfloat16

D_MODEL = 1024
N_META = 16
LANES = 128
ROW_TILE = 128
PAD = ROW_TILE - N_META
ROPE_THETA = 10000.0
NORM_EPS = 1e-6
NEG_BIG = -1e30

ATT_HEAD_DIM = 64
ATT_HEADS = 8
ATT_TQ = 256
Q_SCALE = math.log2(math.e) * ATT_HEAD_DIM ** -0.5
RWKV_HEADS = 16
RWKV_N = 64
RWKV_GN_EPS = 64e-5
RWKV_LORA = 128
HGRN_HEADS = 8
HGRN_K = 128
CHUNK = 64
SEQ_ROWS = 2 * CHUNK
HGRN_LEVELS = 6

Z_W = 15 * D_MODEL
COL_ATT, COL_RWKV, COL_HGRN, COL_MG = 0, 4 * D_MODEL, 8 * D_MODEL, 12 * D_MODEL
IN_TN = 1024
IN_ROW_CHUNKS = 4
VMEM_LIMIT = 56 * 1024 * 1024


def _dot(a, b):
    return jnp.dot(a.astype(BF16), b.astype(BF16), preferred_element_type=F32)


def _dot_nt(a, b):
    return lax.dot_general(a.astype(BF16), b.astype(BF16), (((1,), (1,)), ((), ())),
                           preferred_element_type=F32)


def _dot_tn(a, b):
    return lax.dot_general(a.astype(BF16), b.astype(BF16), (((0,), (0,)), ((), ())),
                           preferred_element_type=F32)


def _split(x):
    hi = x.astype(BF16)
    lo = (x - hi.astype(F32)).astype(BF16)
    return hi, lo


def _dot_x3(a, b):
    ah, al = _split(a)
    bh, bl = _split(b)
    d = functools.partial(jnp.dot, preferred_element_type=F32)
    return d(ah, bh) + d(al, bh) + d(ah, bl)


def _dot_exact_lhs(p_bf16, x):
    xh, xl = _split(x)
    d = functools.partial(jnp.dot, preferred_element_type=F32)
    return d(p_bf16, xh) + d(p_bf16, xl)


def _sigmoid(x):
    return 1.0 / (1.0 + jnp.exp(-x))


def _silu(x):
    return x * _sigmoid(x)


def _inproj_kernel(h_ref, prew_ref, w_ref, wlh_ref, wll_ref, cos_ref, sin_ref, mu_ref, mul_ref,
                   z_ref, zl_ref, u_ref, *, tn):
    j = pl.program_id(1)
    rows = h_ref.shape[0]

    rc = rows // IN_ROW_CHUNKS

    def shift_mix(acc, mu, carry):
        prev = pltpu.roll(acc, 1, 0)
        row = lax.broadcasted_iota(jnp.int32, acc.shape, 0)
        prev = jnp.where(row == 0, carry, prev)
        return acc + (prev - acc) * mu

    @pl.when(j == 0)
    def _():
        x = h_ref[...]
        ms = jnp.mean(x * x, axis=-1, keepdims=True)
        u = x * lax.rsqrt(ms + NORM_EPS) * prew_ref[...]
        u_ref[...] = u.astype(BF16)
        uh, ul = _split(u)
        d = functools.partial(jnp.dot, preferred_element_type=F32)
        zl = d(uh, wlh_ref[...]) + d(ul, wlh_ref[...]) + d(uh, wll_ref[...])
        zl_ref[...] = shift_mix(zl, mul_ref[...], 0.0)

    def chunked(epilogue):
        accs = {}
        for c in range(IN_ROW_CHUNKS + 1):
            if c < IN_ROW_CHUNKS:
                accs[c] = jnp.dot(u_ref[c * rc:(c + 1) * rc, :], w_ref[...], preferred_element_type=F32)
            if c >= 1:
                carry = accs[c - 2][rc - 1:rc, :] if c >= 2 else 0.0
                epilogue(slice((c - 1) * rc, c * rc), accs[c - 1], carry)
                accs.pop(c - 2, None)

    n_q, n_qk = D_MODEL // tn, 2 * D_MODEL // tn
    is_rope = j < n_qk
    is_rwkv = (j >= COL_RWKV // tn) & (j < COL_HGRN // tn)

    @pl.when(is_rope)
    def _():
        scale = jnp.where(j < n_q, Q_SCALE, 1.0).astype(F32)
        lane = lax.broadcasted_iota(jnp.int32, (rc, LANES), 1)

        def epilogue(rs, acc, _):
            cos = cos_ref[rs, :] * scale
            sin = jnp.where(lane < LANES // 2, -sin_ref[rs, :], sin_ref[rs, :]) * scale
            for g in range(tn // LANES):
                x = acc[:, g * LANES:(g + 1) * LANES]
                z_ref[rs, g * LANES:(g + 1) * LANES] = (x * cos + pltpu.roll(x, LANES // 2, 1) * sin).astype(BF16)

        chunked(epilogue)

    @pl.when(is_rwkv)
    def _():
        def epilogue(rs, acc, carry):
            z_ref[rs, :] = shift_mix(acc, mu_ref[...], carry).astype(BF16)

        chunked(epilogue)

    @pl.when(jnp.logical_not(is_rope | is_rwkv))
    def _():
        def epilogue(rs, acc, _):
            z_ref[rs, :] = acc.astype(BF16)

        chunked(epilogue)


def _in_proj(h, pre_w, w_main, wl_hi, wl_lo, cos, sin, mu_main, mu_lora, *, lp):
    t = h.shape[0]
    nb = t // lp
    tn = IN_TN
    return pl.pallas_call(
        functools.partial(_inproj_kernel, tn=tn),
        grid=(nb, Z_W // tn),
        in_specs=[
            pl.BlockSpec((lp, D_MODEL), lambda i, j: (i, 0)),
            pl.BlockSpec((1, D_MODEL), lambda i, j: (0, 0)),
            pl.BlockSpec((D_MODEL, tn), lambda i, j: (0, j)),
            pl.BlockSpec((D_MODEL, RWKV_LORA), lambda i, j: (0, 0)),
            pl.BlockSpec((D_MODEL, RWKV_LORA), lambda i, j: (0, 0)),
            pl.BlockSpec((lp, LANES), lambda i, j: (0, 0)),
            pl.BlockSpec((lp, LANES), lambda i, j: (0, 0)),
            pl.BlockSpec((1, tn), lambda i, j: (0, j)),
            pl.BlockSpec((1, RWKV_LORA), lambda i, j: (0, 0)),
        ],
        out_specs=[
            pl.BlockSpec((lp, tn), lambda i, j: (i, j)),
            pl.BlockSpec((lp, RWKV_LORA), lambda i, j: (i, 0)),
        ],
        out_shape=[jax.ShapeDtypeStruct((t, Z_W), BF16), jax.ShapeDtypeStruct((t, RWKV_LORA), F32)],
        scratch_shapes=[pltpu.VMEM((lp, D_MODEL), BF16)],
        compiler_params=pltpu.CompilerParams(dimension_semantics=("parallel", "arbitrary"),
                                             vmem_limit_bytes=VMEM_LIMIT),
        name="in_proj",
    )(h, pre_w, w_main, wl_hi, wl_lo, cos, sin, mu_main, mu_lora)


def _attn_kernel(lq1_ref, lk1_ref, lq2_ref, lk2_ref, nw_ref, q_ref, k_ref, v_ref, g_ref,
                 o_ref, vx_ref, p_ref, *, lam_init):
    lp = q_ref.shape[0]
    lam = (jnp.exp(jnp.sum(lq1_ref[...] * lk1_ref[...], keepdims=True))
           - jnp.exp(jnp.sum(lq2_ref[...] * lk2_ref[...], keepdims=True)) + lam_init)
    vx_ref[:, :LANES] = v_ref[...]
    vx_ref[:, LANES:] = jnp.ones((lp, LANES), BF16)
    out_scale = nw_ref[...] * (1.0 - lam_init)

    def layout(r0, tq):
        not_pad = lax.broadcasted_iota(jnp.int32, (tq, ROW_TILE), 1) >= PAD
        causal = (lax.broadcasted_iota(jnp.int32, (tq, tq), 1)
                  <= lax.broadcasted_iota(jnp.int32, (tq, tq), 0))
        if r0 == 0:
            return [(0, tq)], [causal & not_pad]
        bounds = [(0, ROW_TILE), (ROW_TILE, r0), (r0, r0 + tq)]
        masks = [not_pad, None, causal]
        if r0 == ROW_TILE:
            del bounds[1], masks[1]
        return bounds, masks

    def scores(r0, tq, half):
        bounds, masks = layout(r0, tq)
        qb = q_ref[r0:r0 + tq, :].astype(F32)
        half1 = (lax.broadcasted_iota(jnp.int32, (tq, LANES), 1) % ATT_HEAD_DIM) < ATT_HEAD_DIM // 2
        qh = (jnp.where(half1, qb, 0.0) if half == 0 else jnp.where(half1, 0.0, qb)).astype(BF16)
        parts = []
        for (c0, c1), mask in zip(bounds, masks):
            s = lax.dot_general(qh, k_ref[c0:c1, :], (((1,), (1,)), ((), ())), preferred_element_type=F32)
            parts.append(s if mask is None else jnp.where(mask, s, NEG_BIG))
        return parts

    def probs(r0, tq, parts, slot):
        bounds, _ = layout(r0, tq)
        cols = [s[:, c:c + LANES] for s in parts for c in range(0, s.shape[1], LANES)]
        m = jnp.max(functools.reduce(jnp.maximum, cols), axis=-1, keepdims=True)
        for s, (c0, c1) in zip(parts, bounds):
            p_ref[slot, 0:tq, c0:c1] = jnp.exp2((s - m).astype(BF16))

    def weighted(r0, tq, slot):
        w = r0 + tq
        pvl = jnp.dot(p_ref[slot, 0:tq, 0:w], vx_ref[0:w, :], preferred_element_type=F32)
        return pvl[:, :LANES] * (1.0 / pvl[:, LANES:])

    def finish(r0, tq, o1, o2):
        o = o1 - lam * o2
        o = o * lax.rsqrt(jnp.mean(o * o, axis=-1, keepdims=True) + NORM_EPS) * out_scale
        o_ref[r0:r0 + tq, :] = (o * _silu(g_ref[r0:r0 + tq, :].astype(F32))).astype(BF16)

    tiles = [(0, ROW_TILE)] + [(r0, ATT_TQ) for r0 in range(ROW_TILE, lp, ATT_TQ)]
    units = [(r0, tq, half) for r0, tq in tiles for half in (0, 1)]
    parts_of, out_of = {}, {}
    for k in range(len(units) + 2):
        if k < len(units):
            parts_of[k] = scores(*units[k])
        if 0 <= k - 1 < len(units):
            r0, tq, _ = units[k - 1]
            probs(r0, tq, parts_of.pop(k - 1), (k - 1) % 2)
        if 0 <= k - 2 < len(units):
            r0, tq, half = units[k - 2]
            out_of[half] = weighted(r0, tq, (k - 2) % 2)
            if half == 1:
                finish(r0, tq, out_of[0], out_of[1])


def _attention(z, lq1, lk1, lq2, lk2, norm_w, *, lp, lam_init):
    t = z.shape[0]
    nb = t // lp
    small = lambda n: pl.BlockSpec((1, n), lambda b, h: (0, 0))
    colblk = lambda base: pl.BlockSpec((lp, LANES), lambda b, h: (b, base + h))
    return pl.pallas_call(
        functools.partial(_attn_kernel, lam_init=lam_init),
        grid=(nb, ATT_HEADS),
        in_specs=[small(ATT_HEAD_DIM)] * 4 + [small(LANES)]
                 + [colblk(0), colblk(ATT_HEADS), colblk(2 * ATT_HEADS), colblk(3 * ATT_HEADS)],
        out_specs=pl.BlockSpec((lp, LANES), lambda b, h: (b, h)),
        out_shape=jax.ShapeDtypeStruct((t, D_MODEL), BF16),
        scratch_shapes=[pltpu.VMEM((lp, 2 * LANES), BF16), pltpu.VMEM((2, ATT_TQ, lp), BF16)],
        compiler_params=pltpu.CompilerParams(dimension_semantics=("parallel", "parallel"),
                                             vmem_limit_bytes=VMEM_LIMIT),
        name="diff_attn",
    )(lq1, lk1, lq2, lk2, norm_w, z, z, z, z)


def _rwkv_kernel(tril_ref, ones_ref, w0_ref, wuh_ref, wul_ref, a0_ref, auh_ref, aul_ref, kk_ref, ka_ref, rk_ref,
                 gnw_ref, gnb_ref, zr_ref, zk_ref, zv_ref, zg_ref, zl_ref, o_ref, ht_ref):
    c = pl.program_id(1)

    @pl.when(c == 0)
    def _():
        ht_ref[...] = jnp.zeros_like(ht_ref)

    n = RWKV_N
    d = functools.partial(jnp.dot, preferred_element_type=F32)

    def lora(x, wh_ref, wl_ref):
        xh, xl = _split(x)
        return d(xh, wh_ref[...]) + d(xl, wh_ref[...]) + d(xh, wl_ref[...])

    def head_sum(x):
        return jnp.concatenate([d(x[:, p * LANES:(p + 1) * LANES].astype(BF16), ones_ref[...])
                                for p in range(x.shape[1] // LANES)], axis=1)

    lane = lax.broadcasted_iota(jnp.int32, (CHUNK, LANES), 1)
    t_idx = lax.broadcasted_iota(jnp.int32, (CHUNK, LANES), 0)
    lo = lane < n
    eye = jnp.where(lane % n == t_idx, 1.0, 0.0).astype(F32)
    t2 = lax.broadcasted_iota(jnp.int32, (2 * CHUNK, LANES), 0)
    s2 = lax.broadcasted_iota(jnp.int32, (2 * CHUNK, LANES), 1) % n
    tri2 = s2 < jnp.where(t2 < CHUNK, t2, t2 - CHUNK + 1)
    blk = ((lax.broadcasted_iota(jnp.int32, (LANES, LANES), 0) < n)
           == (lax.broadcasted_iota(jnp.int32, (LANES, LANES), 1) < n))
    pairs = range(RWKV_HEADS // 2)
    sls = [slice(p * LANES, (p + 1) * LANES) for p in pairs]

    def bd(x):
        return jnp.concatenate([jnp.where(lo, x, 0.0), jnp.where(lo, 0.0, x)], axis=0).astype(BF16)

    def mm(a, b):
        return d(a.astype(BF16), b)

    def local(rs):
        zl = zl_ref[rs, :]
        y = -(w0_ref[...] + lora(jnp.tanh(zl[:, :n]), wuh_ref, wul_ref))
        softplus = jnp.maximum(y, 0.0) + jnp.log(1.0 + jnp.exp(-jnp.abs(y)))
        logw = -jnp.exp(-softplus - 0.5)
        a_all = _sigmoid(a0_ref[...] + lora(zl[:, n:], auh_ref, aul_ref))
        lam_i = _dot_exact_lhs(tril_ref[...], logw)
        r_all, k_all, v_all = (ref[rs, :].astype(F32) for ref in (zr_ref, zk_ref, zv_ref))
        kk = k_all * kk_ref[...]
        kk = kk * lax.rsqrt(jnp.maximum(head_sum(kk * kk), 1e-24))
        k2 = k_all * (1.0 + (a_all - 1.0) * ka_ref[...])
        kb = kk * a_all
        e_ni = jnp.exp(-lam_i)
        rt = r_all * jnp.exp(lam_i)
        at = -kk * jnp.exp(lam_i - logw)
        bt = kb * e_ni
        kt = k2 * e_ni
        lam_end = lam_i[CHUNK - 1:CHUNK, :]
        e_end = jnp.exp(lam_end - lam_i)
        kbe = kb * e_end
        k2e = k2 * e_end
        dec_end = jnp.exp(lam_end)
        bonus = head_sum(r_all * k2 * rk_ref[...]) * v_all
        lhs_ = [jnp.concatenate([at[:, sl], rt[:, sl]], axis=0) for sl in sls]
        mb_ = [jnp.where(tri2, _dot_nt(lhs_[p], bd(bt[:, sls[p]])), 0.0) for p in pairs]
        mk_ = [jnp.where(tri2, _dot_nt(lhs_[p], bd(kt[:, sls[p]])), 0.0) for p in pairs]
        av_ = [mm(mk_[p], bd(v_all[:, sls[p]])) for p in pairs]
        p_ = [mb_[p][:CHUNK] for p in pairs]
        tinv_ = [eye + p_[p] for p in pairs]
        for _ in range(int(math.log2(CHUNK)) - 1):
            p_ = [mm(p_[p], bd(p_[p])) for p in pairs]
            tinv_ = [tinv_[p] + mm(tinv_[p], bd(p_[p])) for p in pairs]
        w_hat_ = [mm(tinv_[p], bd(at[:, sls[p]])) for p in pairs]
        u_hat_ = [mm(tinv_[p], bd(av_[p][:CHUNK])) for p in pairs]
        glhs_ = [jnp.concatenate([w_hat_[p], rt[:, sls[p]]], axis=0).astype(BF16) for p in pairs]
        a_rb_ = [mb_[p][CHUNK:].astype(BF16) for p in pairs]
        rkv_ = [av_[p][CHUNK:] for p in pairs]
        sv_ = [v_all[:, sl].astype(BF16) for sl in sls]
        sk_ = [jnp.concatenate([kbe[:, sl], k2e[:, sl]], axis=0).astype(BF16) for sl in sls]
        dec_ = [dec_end[:, sl] for sl in sls]
        return glhs_, u_hat_, a_rb_, rkv_, sv_, sk_, dec_, bonus

    def carried(rs, glhs_, u_hat_, a_rb_, rkv_, sv_, sk_, dec_, bonus):
        ht_ = [ht_ref[p] for p in pairs]
        g2_ = [_dot_nt(glhs_[p], ht_[p]) for p in pairs]
        u_ = [g2_[p][:CHUNK] + u_hat_[p] for p in pairs]
        o_ = [g2_[p][CHUNK:] + d(a_rb_[p], bd(u_[p])) + rkv_[p] for p in pairs]
        for p in pairs:
            upd = _dot_tn(jnp.concatenate([u_[p].astype(BF16), sv_[p]], axis=0), sk_[p])
            ht_ref[p] = jnp.where(blk, ht_[p] * dec_[p] + upd, 0.0)
        o = jnp.concatenate(o_, axis=1)
        cen = o - head_sum(o) * (1.0 / n)
        var = head_sum(cen * cen) * (1.0 / n)
        o = cen * lax.rsqrt(var + RWKV_GN_EPS) * gnw_ref[...] + gnb_ref[...] + bonus
        o_ref[rs, :] = (o * _silu(zg_ref[rs, :].astype(F32))).astype(BF16)

    chunks = [slice(i * CHUNK, (i + 1) * CHUNK) for i in range(o_ref.shape[0] // CHUNK)]
    parts = [local(rs) for rs in chunks]
    for rs, part in zip(chunks, parts):
        carried(rs, *part)


def _rwkv(z, zl, tril, ones_bd, w0, wup, a0, aup, k_k, k_a, r_k, gn_w, gn_b, *, lp):
    t = z.shape[0]
    nb, nc = t // lp, lp // SEQ_ROWS
    const = lambda a: pl.BlockSpec(a.shape, lambda b, c: (0,) * a.ndim)
    zblk = lambda k: pl.BlockSpec((SEQ_ROWS, D_MODEL), lambda b, c: (b * nc + c, COL_RWKV // D_MODEL + k))
    params = [tril, ones_bd, w0, *_split(wup), a0, *_split(aup), k_k, k_a, r_k, gn_w, gn_b]
    return pl.pallas_call(
        _rwkv_kernel,
        grid=(nb, nc),
        in_specs=[const(a) for a in params] + [zblk(0), zblk(1), zblk(2), zblk(3),
                  pl.BlockSpec((SEQ_ROWS, RWKV_LORA), lambda b, c: (b * nc + c, 0))],
        out_specs=pl.BlockSpec((SEQ_ROWS, D_MODEL), lambda b, c: (b * nc + c, 0)),
        out_shape=jax.ShapeDtypeStruct((t, D_MODEL), BF16),
        scratch_shapes=[pltpu.VMEM((RWKV_HEADS // 2, LANES, LANES), F32)],
        compiler_params=pltpu.CompilerParams(dimension_semantics=("parallel", "arbitrary"),
                                             vmem_limit_bytes=VMEM_LIMIT),
        name="rwkv7",
    )(*params, z, z, z, z, zl)


def _hgrn_plan():
    c = CHUNK
    p = np.zeros((HGRN_LEVELS + 1, c, c), np.float32)
    m = np.zeros((HGRN_LEVELS, c, c), np.float32)
    p[0] = np.tril(np.ones((c, c), np.float32))
    for lvl in range(HGRN_LEVELS):
        b = 1 << lvl
        for t in range(c):
            mid = (t // (2 * b)) * 2 * b + b
            if t >= mid:
                p[lvl + 1, t, mid:t + 1] = 1.0
                m[lvl, t, mid - b:mid] = 1.0
            else:
                p[lvl + 1, t, t + 1:mid] = 1.0
    return p.reshape(-1, c), m


def _hgrn_kernel(p_ref, m_ref, lbraw_ref, nw_ref, zq_ref, zf_ref, zi_ref, zg_ref, o_ref, st_ref, *, layer):
    c = pl.program_id(1)

    @pl.when(c == 0)
    def _():
        st_ref[...] = jnp.zeros_like(st_ref)

    lbraw = lbraw_ref[...]
    e = jnp.exp(lbraw - jnp.max(lbraw, axis=0, keepdims=True))
    sm = e / jnp.sum(e, axis=0, keepdims=True)
    lb = jnp.zeros((1, lbraw.shape[1]), F32)
    for i in range(1, layer + 1):
        lb = lb + sm[i:i + 1]
    row = lax.broadcasted_iota(jnp.int32, (CHUNK, CHUNK), 0)
    col = lax.broadcasted_iota(jnp.int32, (CHUNK, CHUNK), 1)
    eye = row == col
    heads = range(HGRN_HEADS)
    sls = [slice(h * HGRN_K, (h + 1) * HGRN_K) for h in heads]

    def local(rs):
        f = lb + (1.0 - lb) * _sigmoid(zf_ref[rs, :].astype(F32))
        x = _dot_exact_lhs(p_ref[...], jnp.log(f))
        lam_all = x[:CHUNK]
        e_lvl = jnp.exp(x[CHUNK:])
        q_all = _silu(zq_ref[rs, :].astype(F32))
        k_all = 1.0 - f
        q_ = [q_all[:, sl] for sl in sls]
        k_ = [k_all[:, sl] for sl in sls]
        v_ = [zi_ref[rs, sl] for sl in sls]
        lam_ = [lam_all[:, sl] for sl in sls]
        att_ = [jnp.where(eye, jnp.sum(q_[h] * k_[h], axis=-1, keepdims=True), 0.0) for h in heads]
        for lvl in range(HGRN_LEVELS):
            el_ = [e_lvl[lvl * CHUNK:(lvl + 1) * CHUNK, sl] for sl in sls]
            att_ = [att_[h] + m_ref[lvl] * _dot_nt(q_[h] * el_[h], k_[h] * el_[h]) for h in heads]
        intra_ = [_dot(att_[h], v_[h]) for h in heads]
        qe_ = [q_[h] * jnp.exp(lam_[h]) for h in heads]
        lam_end_ = [lam_[h][CHUNK - 1:CHUNK, :] for h in heads]
        kd_ = [k_[h] * jnp.exp(lam_end_[h] - lam_[h]) for h in heads]
        dec_ = [jnp.exp(lam_end_[h]) for h in heads]
        return intra_, qe_, kd_, dec_, v_

    def carried(rs, intra_, qe_, kd_, dec_, v_):
        st_ = [st_ref[h] for h in heads]
        o_ = [intra_[h] + _dot_nt(qe_[h], st_[h]) for h in heads]
        for h in heads:
            st_ref[h] = st_[h] * dec_[h] + _dot_tn(v_[h], kd_[h])
        for h in heads:
            o = o_[h]
            o = o * lax.rsqrt(jnp.mean(o * o, axis=-1, keepdims=True) + NORM_EPS) * nw_ref[...]
            o_ref[rs, sls[h]] = (o * _silu(zg_ref[rs, sls[h]].astype(F32))).astype(BF16)

    chunks = [slice(i * CHUNK, (i + 1) * CHUNK) for i in range(o_ref.shape[0] // CHUNK)]
    parts = [local(rs) for rs in chunks]
    for rs, part in zip(chunks, parts):
        carried(rs, *part)


def _hgrn(z, pmat, masks, lbraw, norm_w, *, lp, layer):
    t = z.shape[0]
    nb, nc = t // lp, lp // SEQ_ROWS
    depth = lbraw.shape[0]
    zblk = lambda k: pl.BlockSpec((SEQ_ROWS, D_MODEL), lambda b, c: (b * nc + c, COL_HGRN // D_MODEL + k))
    return pl.pallas_call(
        functools.partial(_hgrn_kernel, layer=layer),
        grid=(nb, nc),
        in_specs=[pl.BlockSpec(pmat.shape, lambda b, c: (0, 0)),
                  pl.BlockSpec(masks.shape, lambda b, c: (0, 0, 0)),
                  pl.BlockSpec((depth, D_MODEL), lambda b, c: (0, 0)),
                  pl.BlockSpec((1, HGRN_K), lambda b, c: (0, 0)),
                  zblk(0), zblk(1), zblk(2), zblk(3)],
        out_specs=pl.BlockSpec((SEQ_ROWS, D_MODEL), lambda b, c: (b * nc + c, 0)),
        out_shape=jax.ShapeDtypeStruct((t, D_MODEL), BF16),
        scratch_shapes=[pltpu.VMEM((HGRN_HEADS, HGRN_K, HGRN_K), F32)],
        compiler_params=pltpu.CompilerParams(dimension_semantics=("parallel", "arbitrary"),
                                             vmem_limit_bytes=VMEM_LIMIT),
        name="hgrn2",
    )(pmat, masks, lbraw, norm_w, z, z, z, z)


def _merge_kernel(h_ref, oa_ref, or_ref, oh_ref, ga_ref, gr_ref, gh_ref, wa_ref, wr_ref, wh_ref, wo_ref,
                  postw_ref, out_ref):
    d = functools.partial(jnp.dot, preferred_element_type=F32)
    y = (_sigmoid(ga_ref[...].astype(F32)) * d(oa_ref[...], wa_ref[...])
         + _sigmoid(gr_ref[...].astype(F32)) * d(or_ref[...], wr_ref[...])
         + _sigmoid(gh_ref[...].astype(F32)) * d(oh_ref[...], wh_ref[...]))
    y2 = d(y.astype(BF16), wo_ref[...])
    ms = jnp.mean(y2 * y2, axis=-1, keepdims=True)
    out_ref[...] = h_ref[...] + y2 * lax.rsqrt(ms + NORM_EPS) * postw_ref[...]


def _merge(h, o_att, o_rwkv, o_hgrn, z, w_att, w_rwkv, w_hgrn, w_o, post_w, *, tm):
    t = h.shape[0]
    rowblk = pl.BlockSpec((tm, D_MODEL), lambda i: (i, 0))
    gate = lambda k: pl.BlockSpec((tm, D_MODEL), lambda i: (i, COL_MG // D_MODEL + k))
    wblk = pl.BlockSpec((D_MODEL, D_MODEL), lambda i: (0, 0))
    return pl.pallas_call(
        _merge_kernel,
        grid=(t // tm,),
        in_specs=[rowblk, rowblk, rowblk, rowblk, gate(0), gate(1), gate(2), wblk, wblk, wblk, wblk,
                  pl.BlockSpec((1, D_MODEL), lambda i: (0, 0))],
        out_specs=rowblk,
        out_shape=jax.ShapeDtypeStruct((t, D_MODEL), F32),
        compiler_params=pltpu.CompilerParams(dimension_semantics=("parallel",),
                                             vmem_limit_bytes=VMEM_LIMIT),
        name="merge",
    )(h, o_att, o_rwkv, o_hgrn, z, z, z, w_att, w_rwkv, w_hgrn, w_o, post_w)


def _rotary_tables(lp):
    half = ATT_HEAD_DIM // 2
    inv = 1.0 / (ROPE_THETA ** (jnp.arange(0, ATT_HEAD_DIM, 2, dtype=F32) / ATT_HEAD_DIM))
    pos = jnp.maximum(jnp.arange(lp, dtype=F32) - PAD, 0.0)
    ang = pos[:, None] * inv[None, :]
    ang = jnp.concatenate([ang] * (LANES // half), axis=-1)
    return jnp.cos(ang), jnp.sin(ang)


def kernel(x, meta_tokens, pre_norm_w, post_norm_w, w_in, lambda_q1, lambda_k1, lambda_q2, lambda_k2, att_norm_w, rwkv_mu, rwkv_w0, rwkv_w_up, rwkv_a0, rwkv_a_up, rwkv_k_k, rwkv_k_a, rwkv_r_k, rwkv_gn_w, rwkv_gn_b, hgrn_lower_bounds, hgrn_norm_w, w_att_out, w_rwkv_out, w_hgrn_out, w_o):
    nb, seq, d = x.shape
    depth = w_in.shape[0]
    assert d == D_MODEL and seq % ATT_TQ == 0 and meta_tokens.shape == (N_META, D_MODEL)
    lp = seq + ROW_TILE
    t = nb * lp
    lora0 = COL_HGRN
    lora1 = lora0 + RWKV_LORA

    meta = jnp.broadcast_to(meta_tokens[None].astype(F32), (nb, N_META, D_MODEL))
    h = jnp.concatenate([jnp.zeros((nb, PAD, D_MODEL), F32), meta, x.astype(F32)], axis=1).reshape(t, D_MODEL)
    cos, sin = _rotary_tables(lp)
    tril = jnp.asarray(np.tril(np.ones((CHUNK, CHUNK), np.float32)), BF16)
    head_of_lane = np.arange(LANES) // RWKV_N
    ones_bd = jnp.asarray(head_of_lane[:, None] == head_of_lane[None, :], BF16)
    pmat_np, masks_np = _hgrn_plan()
    pmat, masks = jnp.asarray(pmat_np, BF16), jnp.asarray(masks_np, F32)
    row = lambda v: v.reshape(1, -1).astype(F32)

    for l in range(depth):
        half = ATT_HEAD_DIM // 2
        w_qk = w_in[l, :, :2 * D_MODEL].reshape(D_MODEL, -1, 2, 2, half).swapaxes(2, 3).reshape(D_MODEL, -1)
        w_main = jnp.concatenate([w_qk, w_in[l, :, 2 * D_MODEL:lora0], w_in[l, :, lora1:]], axis=1).astype(BF16)
        wl_hi, wl_lo = _split(w_in[l, :, lora0:lora1].astype(F32))
        mu = rwkv_mu[l].astype(F32)
        mu_main = jnp.zeros((1, Z_W), F32).at[0, COL_RWKV:COL_HGRN].set(mu[:4 * D_MODEL])
        z, zl = _in_proj(h, row(pre_norm_w[l]), w_main, wl_hi, wl_lo, cos, sin, mu_main,
                         row(mu[4 * D_MODEL:]), lp=lp)
        lam_init = 0.8 - 0.6 * math.exp(-0.3 * l)
        o_att = _attention(z, row(lambda_q1[l]), row(lambda_k1[l]), row(lambda_q2[l]), row(lambda_k2[l]),
                           row(att_norm_w[l]), lp=lp, lam_init=lam_init)
        o_rwkv = _rwkv(z, zl, tril, ones_bd, row(rwkv_w0[l]), rwkv_w_up[l].astype(F32), row(rwkv_a0[l]),
                       rwkv_a_up[l].astype(F32), row(rwkv_k_k[l]), row(rwkv_k_a[l]), row(rwkv_r_k[l]),
                       row(rwkv_gn_w[l]), row(rwkv_gn_b[l]), lp=lp)
        o_hgrn = _hgrn(z, pmat, masks, hgrn_lower_bounds.astype(F32), row(hgrn_norm_w[l]), lp=lp, layer=l)
        h = _merge(h, o_att, o_rwkv, o_hgrn, z, w_att_out[l].astype(BF16), w_rwkv_out[l].astype(BF16),
                   w_hgrn_out[l].astype(BF16), w_o[l].astype(BF16), row(post_norm_w[l]), tm=lp // 4)
    return h.reshape(nb, lp, D_MODEL)[:, ROW_TILE:].astype(x.dtype)
```

```python
import functools
import math

import numpy as np
import jax
import jax.numpy as jnp
from jax import lax
from jax.experimental import pallas as pl
from jax.experimental.pallas import tpu as pltpu

F32 = jnp.float32
BF16 = jnp.bfloat16

D_MODEL = 1024
N_META = 16
LANES = 128
ROW_TILE = 128
PAD = ROW_TILE - N_META
ROPE_THETA = 10000.0
NORM_EPS = 1e-6
NEG_BIG = -1e30

ATT_HEAD_DIM = 64
ATT_HEADS = 8
ATT_TQ = 256
Q_SCALE = math.log2(math.e) * ATT_HEAD_DIM ** -0.5
RWKV_HEADS = 16
RWKV_N = 64
RWKV_GN_EPS = 64e-5
RWKV_LORA = 128
HGRN_HEADS = 8
HGRN_K = 128
CHUNK = 64
SEQ_ROWS = 2 * CHUNK
HGRN_LEVELS = 6

Z_W = 15 * D_MODEL
COL_ATT, COL_RWKV, COL_HGRN, COL_MG = 0, 4 * D_MODEL, 8 * D_MODEL, 12 * D_MODEL
IN_TN = 1024
IN_ROW_CHUNKS = 4
VMEM_LIMIT = 56 * 1024 * 1024


def _dot(a, b):
    return jnp.dot(a.astype(BF16), b.astype(BF16), preferred_element_type=F32)


def _dot_nt(a, b):
    return lax.dot_general(a.astype(BF16), b.astype(BF16), (((1,), (1,)), ((), ())),
                           preferred_element_type=F32)


def _dot_tn(a, b):
    return lax.dot_general(a.astype(BF16), b.astype(BF16), (((0,), (0,)), ((), ())),
                           preferred_element_type=F32)


def _split(x):
    hi = x.astype(BF16)
    lo = (x - hi.astype(F32)).astype(BF16)
    return hi, lo


def _dot_x3(a, b):
    ah, al = _split(a)
    bh, bl = _split(b)
    d = functools.partial(jnp.dot, preferred_element_type=F32)
    return d(ah, bh) + d(al, bh) + d(ah, bl)


def _dot_exact_lhs(p_bf16, x):
    xh, xl = _split(x)
    d = functools.partial(jnp.dot, preferred_element_type=F32)
    return d(p_bf16, xh) + d(p_bf16, xl)


def _sigmoid(x):
    return 1.0 / (1.0 + jnp.exp(-x))


def _silu(x):
    return x * _sigmoid(x)


def _inproj_kernel(h_ref, prew_ref, w_ref, wlh_ref, wll_ref, cos_ref, sin_ref, mu_ref, mul_ref,
                   z_ref, zl_ref, u_ref, *, tn):
    j = pl.program_id(1)
    rows = h_ref.shape[0]

    rc = rows // IN_ROW_CHUNKS

    def shift_mix(acc, mu, carry):
        prev = pltpu.roll(acc, 1, 0)
        row = lax.broadcasted_iota(jnp.int32, acc.shape, 0)
        prev = jnp.where(row == 0, carry, prev)
        return acc + (prev - acc) * mu

    @pl.when(j == 0)
    def _():
        x = h_ref[...]
        ms = jnp.mean(x * x, axis=-1, keepdims=True)
        u = x * lax.rsqrt(ms + NORM_EPS) * prew_ref[...]
        u_ref[...] = u.astype(BF16)
        uh, ul = _split(u)
        d = functools.partial(jnp.dot, preferred_element_type=F32)
        zl = d(uh, wlh_ref[...]) + d(ul, wlh_ref[...]) + d(uh, wll_ref[...])
        zl_ref[...] = shift_mix(zl, mul_ref[...], 0.0)

    def chunked(epilogue):
        accs = {}
        for c in range(IN_ROW_CHUNKS + 1):
            if c < IN_ROW_CHUNKS:
                accs[c] = jnp.dot(u_ref[c * rc:(c + 1) * rc, :], w_ref[...], preferred_element_type=F32)
            if c >= 1:
                carry = accs[c - 2][rc - 1:rc, :] if c >= 2 else 0.0
                epilogue(slice((c - 1) * rc, c * rc), accs[c - 1], carry)
                accs.pop(c - 2, None)

    n_q, n_qk = D_MODEL // tn, 2 * D_MODEL // tn
    is_rope = j < n_qk
    is_rwkv = (j >= COL_RWKV // tn) & (j < COL_HGRN // tn)

    @pl.when(is_rope)
    def _():
        scale = jnp.where(j < n_q, Q_SCALE, 1.0).astype(F32)
        lane = lax.broadcasted_iota(jnp.int32, (rc, LANES), 1)

        def epilogue(rs, acc, _):
            cos = cos_ref[rs, :] * scale
            sin = jnp.where(lane < LANES // 2, -sin_ref[rs, :], sin_ref[rs, :]) * scale
            for g in range(tn // LANES):
                x = acc[:, g * LANES:(g + 1) * LANES]
                z_ref[rs, g * LANES:(g + 1) * LANES] = (x * cos + pltpu.roll(x, LANES // 2, 1) * sin).astype(BF16)

        chunked(epilogue)

    @pl.when(is_rwkv)
    def _():
        def epilogue(rs, acc, carry):
            z_ref[rs, :] = shift_mix(acc, mu_ref[...], carry).astype(BF16)

        chunked(epilogue)

    @pl.when(jnp.logical_not(is_rope | is_rwkv))
    def _():
        def epilogue(rs, acc, _):
            z_ref[rs, :] = acc.astype(BF16)

        chunked(epilogue)


def _in_proj(h, pre_w, w_main, wl_hi, wl_lo, cos, sin, mu_main, mu_lora, *, lp):
    t = h.shape[0]
    nb = t // lp
    tn = IN_TN
    return pl.pallas_call(
        functools.partial(_inproj_kernel, tn=tn),
        grid=(nb, Z_W // tn),
        in_specs=[
            pl.BlockSpec((lp, D_MODEL), lambda i, j: (i, 0)),
            pl.BlockSpec((1, D_MODEL), lambda i, j: (0, 0)),
            pl.BlockSpec((D_MODEL, tn), lambda i, j: (0, j)),
            pl.BlockSpec((D_MODEL, RWKV_LORA), lambda i, j: (0, 0)),
            pl.BlockSpec((D_MODEL, RWKV_LORA), lambda i, j: (0, 0)),
            pl.BlockSpec((lp, LANES), lambda i, j: (0, 0)),
            pl.BlockSpec((lp, LANES), lambda i, j: (0, 0)),
            pl.BlockSpec((1, tn), lambda i, j: (0, j)),
            pl.BlockSpec((1, RWKV_LORA), lambda i, j: (0, 0)),
        ],
        out_specs=[
            pl.BlockSpec((lp, tn), lambda i, j: (i, j)),
            pl.BlockSpec((lp, RWKV_LORA), lambda i, j: (i, 0)),
        ],
        out_shape=[jax.ShapeDtypeStruct((t, Z_W), BF16), jax.ShapeDtypeStruct((t, RWKV_LORA), F32)],
        scratch_shapes=[pltpu.VMEM((lp, D_MODEL), BF16)],
        compiler_params=pltpu.CompilerParams(dimension_semantics=("parallel", "arbitrary"),
                                             vmem_limit_bytes=VMEM_LIMIT),
        name="in_proj",
    )(h, pre_w, w_main, wl_hi, wl_lo, cos, sin, mu_main, mu_lora)


def _attn_tiles(lp):
    return [(r0, min(ATT_TQ, lp - r0)) for r0 in range(0, lp, ATT_TQ)]


def _attn_kernel(lq1_ref, lk1_ref, lq2_ref, lk2_ref, nw_ref, q_ref, k_ref, v_ref, g_ref,
                 o_ref, vx_ref, *p_refs, lam_init):
    lp = q_ref.shape[0]
    lam = (jnp.exp(jnp.sum(lq1_ref[...] * lk1_ref[...], keepdims=True))
           - jnp.exp(jnp.sum(lq2_ref[...] * lk2_ref[...], keepdims=True)) + lam_init)
    vx_ref[:, :LANES] = v_ref[...]
    vx_ref[:, LANES:] = jnp.ones((lp, LANES), BF16)
    out_scale = nw_ref[...] * (1.0 - lam_init)

    def layout(r0, tq):
        col = lax.broadcasted_iota(jnp.int32, (2 * tq, tq), 1)
        causal = col <= lax.broadcasted_iota(jnp.int32, (2 * tq, tq), 0) % tq
        if r0 == 0:
            return [(0, tq)], [causal & (col >= PAD)]
        not_pad = lax.broadcasted_iota(jnp.int32, (2 * tq, ROW_TILE), 1) >= PAD
        return [(0, ROW_TILE), (ROW_TILE, r0), (r0, r0 + tq)], [not_pad, None, causal]

    def scores(r0, tq):
        bounds, masks = layout(r0, tq)
        qb = q_ref[r0:r0 + tq, :].astype(F32)
        half1 = (lax.broadcasted_iota(jnp.int32, (tq, LANES), 1) % ATT_HEAD_DIM) < ATT_HEAD_DIM // 2
        qh = jnp.concatenate([jnp.where(half1, qb, 0.0), jnp.where(half1, 0.0, qb)], axis=0).astype(BF16)
        parts = []
        for (c0, c1), mask in zip(bounds, masks):
            s = lax.dot_general(qh, k_ref[c0:c1, :], (((1,), (1,)), ((), ())), preferred_element_type=F32)
            parts.append(s if mask is None else jnp.where(mask, s, NEG_BIG))
        return parts

    def probs(r0, tq, parts, p_ref):
        bounds, _ = layout(r0, tq)
        cols = [s[:, c:c + LANES] for s in parts for c in range(0, s.shape[1], LANES)]
        m = jnp.max(functools.reduce(jnp.maximum, cols), axis=-1, keepdims=True)
        for s, (c0, c1) in zip(parts, bounds):
            p_ref[:, c0:c1] = jnp.exp2((s - m).astype(BF16))

    def weighted(r0, tq, p_ref):
        pvl = jnp.dot(p_ref[...], vx_ref[0:r0 + tq, :], preferred_element_type=F32)
        o12 = pvl[:, :LANES] * (1.0 / pvl[:, LANES:])
        o = o12[:tq] - lam * o12[tq:]
        o = o * lax.rsqrt(jnp.mean(o * o, axis=-1, keepdims=True) + NORM_EPS) * out_scale
        o_ref[r0:r0 + tq, :] = (o * _silu(g_ref[r0:r0 + tq, :].astype(F32))).astype(BF16)

    tiles = _attn_tiles(lp)
    parts_of = {}
    for k in range(len(tiles) + 2):
        if k < len(tiles):
            parts_of[k] = scores(*tiles[k])
        if 0 <= k - 1 < len(tiles):
            probs(*tiles[k - 1], parts_of.pop(k - 1), p_refs[k - 1])
        if 0 <= k - 2 < len(tiles):
            weighted(*tiles[k - 2], p_refs[k - 2])


def _attention(z, lq1, lk1, lq2, lk2, norm_w, *, lp, lam_init):
    t = z.shape[0]
    nb = t // lp
    small = lambda n: pl.BlockSpec((1, n), lambda b, h: (0, 0))
    colblk = lambda base: pl.BlockSpec((lp, LANES), lambda b, h: (b, base + h))
    return pl.pallas_call(
        functools.partial(_attn_kernel, lam_init=lam_init),
        grid=(nb, ATT_HEADS),
        in_specs=[small(ATT_HEAD_DIM)] * 4 + [small(LANES)]
                 + [colblk(0), colblk(ATT_HEADS), colblk(2 * ATT_HEADS), colblk(3 * ATT_HEADS)],
        out_specs=pl.BlockSpec((lp, LANES), lambda b, h: (b, h)),
        out_shape=jax.ShapeDtypeStruct((t, D_MODEL), BF16),
        scratch_shapes=[pltpu.VMEM((lp, 2 * LANES), BF16)]
                       + [pltpu.VMEM((2 * tq, r0 + tq), BF16) for r0, tq in _attn_tiles(lp)],
        compiler_params=pltpu.CompilerParams(dimension_semantics=("parallel", "parallel"),
                                             vmem_limit_bytes=VMEM_LIMIT),
        name="diff_attn",
    )(lq1, lk1, lq2, lk2, norm_w, z, z, z, z)


def _rwkv_kernel(tril_ref, ones_ref, w0_ref, wuh_ref, wul_ref, a0_ref, auh_ref, aul_ref, kk_ref, ka_ref, rk_ref,
                 gnw_ref, gnb_ref, zr_ref, zk_ref, zv_ref, zg_ref, zl_ref, o_ref, ht_ref):
    c = pl.program_id(1)

    @pl.when(c == 0)
    def _():
        ht_ref[...] = jnp.zeros_like(ht_ref)

    n = RWKV_N
    d = functools.partial(jnp.dot, preferred_element_type=F32)

    def lora(x, wh_ref, wl_ref):
        xh, xl = _split(x)
        return d(xh, wh_ref[...]) + d(xl, wh_ref[...]) + d(xh, wl_ref[...])

    def head_sum(x):
        r, nblk = x.shape[0], x.shape[1] // LANES
        stacked = jnp.concatenate([x[:, p * LANES:(p + 1) * LANES] for p in range(nblk)], axis=0)
        s = d(stacked.astype(BF16), ones_ref[...])
        return jnp.concatenate([s[p * r:(p + 1) * r] for p in range(nblk)], axis=1)

    lane = lax.broadcasted_iota(jnp.int32, (CHUNK, LANES), 1)
    t_idx = lax.broadcasted_iota(jnp.int32, (CHUNK, LANES), 0)
    lo = lane < n
    eye = jnp.where(lane % n == t_idx, 1.0, 0.0).astype(F32)
    t2 = lax.broadcasted_iota(jnp.int32, (2 * CHUNK, LANES), 0)
    s2 = lax.broadcasted_iota(jnp.int32, (2 * CHUNK, LANES), 1) % n
    tri2 = s2 < jnp.where(t2 < CHUNK, t2, t2 - CHUNK + 1)
    blk = ((lax.broadcasted_iota(jnp.int32, (LANES, LANES), 0) < n)
           == (lax.broadcasted_iota(jnp.int32, (LANES, LANES), 1) < n))
    pairs = range(RWKV_HEADS // 2)
    sls = [slice(p * LANES, (p + 1) * LANES) for p in pairs]

    def bd(x):
        return jnp.concatenate([jnp.where(lo, x, 0.0), jnp.where(lo, 0.0, x)], axis=0).astype(BF16)

    def mm(a, b):
        return d(a.astype(BF16), b)

    nch = o_ref.shape[0] // CHUNK
    chunks = [slice(i * CHUNK, (i + 1) * CHUNK) for i in range(nch)]
    units = [(rs, sl) for rs in chunks for sl in sls]
    idx = range(len(units))
    zl = zl_ref[...]
    y = -(w0_ref[...] + lora(jnp.tanh(zl[:, :n]), wuh_ref, wul_ref))
    softplus = jnp.maximum(y, 0.0) + jnp.log(1.0 + jnp.exp(-jnp.abs(y)))
    logw = -jnp.exp(-softplus - 0.5)
    a_all = _sigmoid(a0_ref[...] + lora(zl[:, n:], auh_ref, aul_ref))
    lam_i = _dot_exact_lhs(tril_ref[...], logw)
    r_all, k_all, v_all = (ref[...].astype(F32) for ref in (zr_ref, zk_ref, zv_ref))
    kk = k_all * kk_ref[...]
    kk = kk * lax.rsqrt(jnp.maximum(head_sum(kk * kk), 1e-24))
    k2 = k_all * (1.0 + (a_all - 1.0) * ka_ref[...])
    kb = kk * a_all
    e_ni = jnp.exp(-lam_i)
    rt = r_all * jnp.exp(lam_i)
    at = -kk * jnp.exp(lam_i - logw)
    bt = kb * e_ni
    kt = k2 * e_ni
    bonus = head_sum(r_all * k2 * rk_ref[...]) * v_all
    lam_end_ = [lam_i[rs.stop - 1:rs.stop, :] for rs in chunks]
    e_end = jnp.concatenate([jnp.exp(lam_end_[c] - lam_i[chunks[c], :]) for c in range(nch)], axis=0)
    kbe = kb * e_end
    k2e = k2 * e_end
    dec_ = [jnp.exp(lam_end_[c][:, sl]) for c in range(nch) for sl in sls]

    lhs_ = [jnp.concatenate([at[rs, sl], rt[rs, sl]], axis=0) for rs, sl in units]
    m2_ = [_dot_nt(lhs_[u], jnp.concatenate([bd(bt[rs, sl]), bd(kt[rs, sl])], axis=0))
           for u, (rs, sl) in enumerate(units)]
    mb_ = [jnp.where(tri2, m2_[u][:, :LANES], 0.0) for u in idx]
    mk_ = [jnp.where(tri2, m2_[u][:, LANES:], 0.0) for u in idx]
    av_ = [mm(mk_[u], bd(v_all[rs, sl])) for u, (rs, sl) in enumerate(units)]
    tinv_ = [eye + mb_[u][:CHUNK] for u in idx]
    p_ = [mm(mb_[u][:CHUNK], bd(mb_[u][:CHUNK])) for u in idx]
    for _ in range(int(math.log2(CHUNK)) - 2):
        tp_ = [mm(jnp.concatenate([tinv_[u], p_[u]], axis=0), bd(p_[u])) for u in idx]
        tinv_ = [tinv_[u] + tp_[u][:CHUNK] for u in idx]
        p_ = [tp_[u][CHUNK:] for u in idx]
    tinv_ = [tinv_[u] + mm(tinv_[u], bd(p_[u])) for u in idx]
    wu_ = [mm(tinv_[u], jnp.concatenate([bd(at[rs, sl]), bd(av_[u][:CHUNK])], axis=1))
           for u, (rs, sl) in enumerate(units)]
    glhs_ = [jnp.concatenate([wu_[u][:, :LANES], rt[rs, sl]], axis=0).astype(BF16)
             for u, (rs, sl) in enumerate(units)]
    a_rb_ = [mb_[u][CHUNK:].astype(BF16) for u in idx]
    sv_ = [v_all[rs, sl].astype(BF16) for rs, sl in units]
    sk_ = [jnp.concatenate([kbe[rs, sl], k2e[rs, sl]], axis=0).astype(BF16) for rs, sl in units]

    o_chunks = []
    for c in range(nch):
        us = [c * len(sls) + p for p in pairs]
        ht_ = [ht_ref[p] for p in pairs]
        g2_ = [_dot_nt(glhs_[u], ht_[p]) for p, u in zip(pairs, us)]
        u_ = [g2_[p][:CHUNK] + wu_[u][:, LANES:] for p, u in zip(pairs, us)]
        o_ = [g2_[p][CHUNK:] + d(a_rb_[u], bd(u_[p])) + av_[u][CHUNK:] for p, u in zip(pairs, us)]
        for p, u in zip(pairs, us):
            upd = _dot_tn(jnp.concatenate([u_[p].astype(BF16), sv_[u]], axis=0), sk_[u])
            ht_ref[p] = jnp.where(blk, ht_[p] * dec_[u] + upd, 0.0)
        o_chunks.append(jnp.concatenate(o_, axis=1))
    o = jnp.concatenate(o_chunks, axis=0)
    cen = o - head_sum(o) * (1.0 / n)
    var = head_sum(cen * cen) * (1.0 / n)
    o = cen * lax.rsqrt(var + RWKV_GN_EPS) * gnw_ref[...] + gnb_ref[...] + bonus
    o_ref[...] = (o * _silu(zg_ref[...].astype(F32))).astype(BF16)


def _rwkv(z, zl, tril, ones_bd, w0, wup, a0, aup, k_k, k_a, r_k, gn_w, gn_b, *, lp):
    t = z.shape[0]
    nb, nc = t // lp, lp // SEQ_ROWS
    const = lambda a: pl.BlockSpec(a.shape, lambda b, c: (0,) * a.ndim)
    zblk = lambda k: pl.BlockSpec((SEQ_ROWS, D_MODEL), lambda b, c: (b * nc + c, COL_RWKV // D_MODEL + k))
    params = [tril, ones_bd, w0, *_split(wup), a0, *_split(aup), k_k, k_a, r_k, gn_w, gn_b]
    return pl.pallas_call(
        _rwkv_kernel,
        grid=(nb, nc),
        in_specs=[const(a) for a in params] + [zblk(0), zblk(1), zblk(2), zblk(3),
                  pl.BlockSpec((SEQ_ROWS, RWKV_LORA), lambda b, c: (b * nc + c, 0))],
        out_specs=pl.BlockSpec((SEQ_ROWS, D_MODEL), lambda b, c: (b * nc + c, 0)),
        out_shape=jax.ShapeDtypeStruct((t, D_MODEL), BF16),
        scratch_shapes=[pltpu.VMEM((RWKV_HEADS // 2, LANES, LANES), F32)],
        compiler_params=pltpu.CompilerParams(dimension_semantics=("parallel", "arbitrary"),
                                             vmem_limit_bytes=VMEM_LIMIT),
        name="rwkv7",
    )(*params, z, z, z, z, zl)


def _hgrn_plan():
    c = CHUNK
    p = np.zeros((HGRN_LEVELS + 1, c, c), np.float32)
    m = np.zeros((HGRN_LEVELS, c, c), np.float32)
    p[0] = np.tril(np.ones((c, c), np.float32))
    for lvl in range(HGRN_LEVELS):
        b = 1 << lvl
        for t in range(c):
            mid = (t // (2 * b)) * 2 * b + b
            if t >= mid:
                p[lvl + 1, t, mid:t + 1] = 1.0
                m[lvl, t, mid - b:mid] = 1.0
            else:
                p[lvl + 1, t, t + 1:mid] = 1.0
    return p.reshape(-1, c), m


def _hgrn_kernel(p_ref, m_ref, lbraw_ref, nw_ref, zq_ref, zf_ref, zi_ref, zg_ref, o_ref, st_ref, *, layer):
    c = pl.program_id(1)

    @pl.when(c == 0)
    def _():
        st_ref[...] = jnp.zeros_like(st_ref)

    lbraw = lbraw_ref[...]
    e = jnp.exp(lbraw - jnp.max(lbraw, axis=0, keepdims=True))
    sm = e / jnp.sum(e, axis=0, keepdims=True)
    lb = jnp.zeros((1, lbraw.shape[1]), F32)
    for i in range(1, layer + 1):
        lb = lb + sm[i:i + 1]
    row = lax.broadcasted_iota(jnp.int32, (CHUNK, CHUNK), 0)
    col = lax.broadcasted_iota(jnp.int32, (CHUNK, CHUNK), 1)
    eye = row == col
    heads = range(HGRN_HEADS)
    sls = [slice(h * HGRN_K, (h + 1) * HGRN_K) for h in heads]

    nch = o_ref.shape[0] // CHUNK
    chunks = [slice(i * CHUNK, (i + 1) * CHUNK) for i in range(nch)]
    units = [(c, sl) for c in range(nch) for sl in sls]
    idx = range(len(units))
    f = lb + (1.0 - lb) * _sigmoid(zf_ref[...].astype(F32))
    logf = jnp.log(f)
    x_ = [_dot_exact_lhs(p_ref[...], logf[rs, :]) for rs in chunks]
    e_lvl_ = [jnp.exp(x[CHUNK:]) for x in x_]
    q_all = _silu(zq_ref[...].astype(F32))
    k_all = 1.0 - f
    q_ = [q_all[chunks[c], sl] for c, sl in units]
    k_ = [k_all[chunks[c], sl] for c, sl in units]
    v_ = [zi_ref[chunks[c], sl] for c, sl in units]
    lam_ = [x_[c][:CHUNK, sl] for c, sl in units]
    att_ = [jnp.where(eye, jnp.sum(q_[u] * k_[u], axis=-1, keepdims=True), 0.0) for u in idx]
    for lvl in range(HGRN_LEVELS):
        el_ = [e_lvl_[c][lvl * CHUNK:(lvl + 1) * CHUNK, sl] for c, sl in units]
        att_ = [att_[u] + m_ref[lvl] * _dot_nt(q_[u] * el_[u], k_[u] * el_[u]) for u in idx]
    intra_ = [_dot(att_[u], v_[u]) for u in idx]
    qe_ = [q_[u] * jnp.exp(lam_[u]) for u in idx]
    lam_end_ = [lam_[u][CHUNK - 1:CHUNK, :] for u in idx]
    kd_ = [k_[u] * jnp.exp(lam_end_[u] - lam_[u]) for u in idx]
    dec_ = [jnp.exp(lam_end_[u]) for u in idx]

    for c in range(nch):
        us = [c * HGRN_HEADS + h for h in heads]
        st_ = [st_ref[h] for h in heads]
        o_ = [intra_[u] + _dot_nt(qe_[u], st_[h]) for h, u in zip(heads, us)]
        for h, u in zip(heads, us):
            st_ref[h] = st_[h] * dec_[u] + _dot_tn(v_[u], kd_[u])
        for h in heads:
            o = o_[h]
            o = o * lax.rsqrt(jnp.mean(o * o, axis=-1, keepdims=True) + NORM_EPS) * nw_ref[...]
            o_ref[chunks[c], sls[h]] = (o * _silu(zg_ref[chunks[c], sls[h]].astype(F32))).astype(BF16)


def _hgrn(z, pmat, masks, lbraw, norm_w, *, lp, layer):
    t = z.shape[0]
    nb, nc = t // lp, lp // SEQ_ROWS
    depth = lbraw.shape[0]
    zblk = lambda k: pl.BlockSpec((SEQ_ROWS, D_MODEL), lambda b, c: (b * nc + c, COL_HGRN // D_MODEL + k))
    return pl.pallas_call(
        functools.partial(_hgrn_kernel, layer=layer),
        grid=(nb, nc),
        in_specs=[pl.BlockSpec(pmat.shape, lambda b, c: (0, 0)),
                  pl.BlockSpec(masks.shape, lambda b, c: (0, 0, 0)),
                  pl.BlockSpec((depth, D_MODEL), lambda b, c: (0, 0)),
                  pl.BlockSpec((1, HGRN_K), lambda b, c: (0, 0)),
                  zblk(0), zblk(1), zblk(2), zblk(3)],
        out_specs=pl.BlockSpec((SEQ_ROWS, D_MODEL), lambda b, c: (b * nc + c, 0)),
        out_shape=jax.ShapeDtypeStruct((t, D_MODEL), BF16),
        scratch_shapes=[pltpu.VMEM((HGRN_HEADS, HGRN_K, HGRN_K), F32)],
        compiler_params=pltpu.CompilerParams(dimension_semantics=("parallel", "arbitrary"),
                                             vmem_limit_bytes=VMEM_LIMIT),
        name="hgrn2",
    )(pmat, masks, lbraw, norm_w, z, z, z, z)


def _merge_kernel(h_ref, oa_ref, or_ref, oh_ref, ga_ref, gr_ref, gh_ref, wa_ref, wr_ref, wh_ref, wo_ref,
                  postw_ref, out_ref):
    d = functools.partial(jnp.dot, preferred_element_type=F32)
    y = (_sigmoid(ga_ref[...].astype(F32)) * d(oa_ref[...], wa_ref[...])
         + _sigmoid(gr_ref[...].astype(F32)) * d(or_ref[...], wr_ref[...])
         + _sigmoid(gh_ref[...].astype(F32)) * d(oh_ref[...], wh_ref[...]))
    y2 = d(y.astype(BF16), wo_ref[...])
    ms = jnp.mean(y2 * y2, axis=-1, keepdims=True)
    out_ref[...] = h_ref[...] + y2 * lax.rsqrt(ms + NORM_EPS) * postw_ref[...]


def _merge(h, o_att, o_rwkv, o_hgrn, z, w_att, w_rwkv, w_hgrn, w_o, post_w, *, tm):
    t = h.shape[0]
    rowblk = pl.BlockSpec((tm, D_MODEL), lambda i: (i, 0))
    gate = lambda k: pl.BlockSpec((tm, D_MODEL), lambda i: (i, COL_MG // D_MODEL + k))
    wblk = pl.BlockSpec((D_MODEL, D_MODEL), lambda i: (0, 0))
    return pl.pallas_call(
        _merge_kernel,
        grid=(t // tm,),
        in_specs=[rowblk, rowblk, rowblk, rowblk, gate(0), gate(1), gate(2), wblk, wblk, wblk, wblk,
                  pl.BlockSpec((1, D_MODEL), lambda i: (0, 0))],
        out_specs=rowblk,
        out_shape=jax.ShapeDtypeStruct((t, D_MODEL), F32),
        compiler_params=pltpu.CompilerParams(dimension_semantics=("parallel",),
                                             vmem_limit_bytes=VMEM_LIMIT),
        name="merge",
    )(h, o_att, o_rwkv, o_hgrn, z, z, z, w_att, w_rwkv, w_hgrn, w_o, post_w)


def _rotary_tables(lp):
    half = ATT_HEAD_DIM // 2
    inv = 1.0 / (ROPE_THETA ** (jnp.arange(0, ATT_HEAD_DIM, 2, dtype=F32) / ATT_HEAD_DIM))
    pos = jnp.maximum(jnp.arange(lp, dtype=F32) - PAD, 0.0)
    ang = pos[:, None] * inv[None, :]
    ang = jnp.concatenate([ang] * (LANES // half), axis=-1)
    return jnp.cos(ang), jnp.sin(ang)


def kernel(x, meta_tokens, pre_norm_w, post_norm_w, w_in, lambda_q1, lambda_k1, lambda_q2, lambda_k2, att_norm_w, rwkv_mu, rwkv_w0, rwkv_w_up, rwkv_a0, rwkv_a_up, rwkv_k_k, rwkv_k_a, rwkv_r_k, rwkv_gn_w, rwkv_gn_b, hgrn_lower_bounds, hgrn_norm_w, w_att_out, w_rwkv_out, w_hgrn_out, w_o):
    nb, seq, d = x.shape
    depth = w_in.shape[0]
    assert d == D_MODEL and seq % ATT_TQ == 0 and meta_tokens.shape == (N_META, D_MODEL)
    lp = seq + ROW_TILE
    t = nb * lp
    lora0 = COL_HGRN
    lora1 = lora0 + RWKV_LORA

    meta = jnp.broadcast_to(meta_tokens[None].astype(F32), (nb, N_META, D_MODEL))
    h = jnp.concatenate([jnp.zeros((nb, PAD, D_MODEL), F32), meta, x.astype(F32)], axis=1).reshape(t, D_MODEL)
    cos, sin = _rotary_tables(lp)
    tril = jnp.asarray(np.kron(np.eye(SEQ_ROWS // CHUNK), np.tril(np.ones((CHUNK, CHUNK)))), BF16)
    head_of_lane = np.arange(LANES) // RWKV_N
    ones_bd = jnp.asarray(head_of_lane[:, None] == head_of_lane[None, :], BF16)
    pmat_np, masks_np = _hgrn_plan()
    pmat, masks = jnp.asarray(pmat_np, BF16), jnp.asarray(masks_np, F32)
    row = lambda v: v.reshape(1, -1).astype(F32)

    for l in range(depth):
        half = ATT_HEAD_DIM // 2
        w_qk = w_in[l, :, :2 * D_MODEL].reshape(D_MODEL, -1, 2, 2, half).swapaxes(2, 3).reshape(D_MODEL, -1)
        w_main = jnp.concatenate([w_qk, w_in[l, :, 2 * D_MODEL:lora0], w_in[l, :, lora1:]], axis=1).astype(BF16)
        wl_hi, wl_lo = _split(w_in[l, :, lora0:lora1].astype(F32))
        mu = rwkv_mu[l].astype(F32)
        mu_main = jnp.zeros((1, Z_W), F32).at[0, COL_RWKV:COL_HGRN].set(mu[:4 * D_MODEL])
        z, zl = _in_proj(h, row(pre_norm_w[l]), w_main, wl_hi, wl_lo, cos, sin, mu_main,
                         row(mu[4 * D_MODEL:]), lp=lp)
        lam_init = 0.8 - 0.6 * math.exp(-0.3 * l)
        o_att = _attention(z, row(lambda_q1[l]), row(lambda_k1[l]), row(lambda_q2[l]), row(lambda_k2[l]),
                           row(att_norm_w[l]), lp=lp, lam_init=lam_init)
        o_rwkv = _rwkv(z, zl, tril, ones_bd, row(rwkv_w0[l]), rwkv_w_up[l].astype(F32), row(rwkv_a0[l]),
                       rwkv_a_up[l].astype(F32), row(rwkv_k_k[l]), row(rwkv_k_a[l]), row(rwkv_r_k[l]),
                       row(rwkv_gn_w[l]), row(rwkv_gn_b[l]), lp=lp)
        o_hgrn = _hgrn(z, pmat, masks, hgrn_lower_bounds.astype(F32), row(hgrn_norm_w[l]), lp=lp, layer=l)
        h = _merge(h, o_att, o_rwkv, o_hgrn, z, w_att_out[l].astype(BF16), w_rwkv_out[l].astype(BF16),
                   w_hgrn_out[l].astype(BF16), w_o[l].astype(BF16), row(post_norm_w[l]), tm=lp // 4)
    return h.reshape(nb, lp, D_MODEL)[:, ROW_TILE:].astype(x.dtype)
```

```python
import functools
import math

import numpy as np
import jax
import jax.numpy as jnp
from jax import lax
from jax.experimental import pallas as pl
from jax.experimental.pallas import tpu as pltpu

F32 = jnp.float32
BF16 = jnp.bfloat16

D_MODEL = 1024
N_META = 16
LANES = 128
ROW_TILE = 128
PAD = ROW_TILE - N_META
ROPE_THETA = 10000.0
NORM_EPS = 1e-6
NEG_BIG = -1e30

ATT_HEAD_DIM = 64
ATT_HEADS = 8
ATT_TQ = 256
Q_SCALE = math.log2(math.e) * ATT_HEAD_DIM ** -0.5
RWKV_HEADS = 16
RWKV_N = 64
RWKV_GN_EPS = 64e-5
RWKV_LORA = 128
HGRN_HEADS = 8
HGRN_K = 128
CHUNK = 64
SEQ_ROWS = 2 * CHUNK
HGRN_LEVELS = 6

Z_W = 15 * D_MODEL
COL_ATT, COL_RWKV, COL_HGRN, COL_MG = 0, 4 * D_MODEL, 8 * D_MODEL, 12 * D_MODEL
IN_TN = 1024
IN_ROW_CHUNKS = 4
VMEM_LIMIT = 56 * 1024 * 1024


def _dot(a, b):
    return jnp.dot(a.astype(BF16), b.astype(BF16), preferred_element_type=F32)


def _dot_nt(a, b):
    return lax.dot_general(a.astype(BF16), b.astype(BF16), (((1,), (1,)), ((), ())),
                           preferred_element_type=F32)


def _dot_tn(a, b):
    return lax.dot_general(a.astype(BF16), b.astype(BF16), (((0,), (0,)), ((), ())),
                           preferred_element_type=F32)


def _split(x):
    hi = x.astype(BF16)
    lo = (x - hi.astype(F32)).astype(BF16)
    return hi, lo


def _dot_x3(a, b):
    ah, al = _split(a)
    bh, bl = _split(b)
    d = functools.partial(jnp.dot, preferred_element_type=F32)
    return d(ah, bh) + d(al, bh) + d(ah, bl)


def _dot_exact_lhs(p_bf16, x):
    xh, xl = _split(x)
    d = functools.partial(jnp.dot, preferred_element_type=F32)
    return d(p_bf16, xh) + d(p_bf16, xl)


def _sigmoid(x):
    return 1.0 / (1.0 + jnp.exp(-x))


def _silu(x):
    return x * _sigmoid(x)


def _inproj_kernel(h_ref, prew_ref, w_ref, wlh_ref, wll_ref, cos_ref, sin_ref, mu_ref, mul_ref,
                   z_ref, zl_ref, u_ref, *, tn):
    j = pl.program_id(1)
    rows = h_ref.shape[0]

    rc = rows // IN_ROW_CHUNKS

    def shift_mix(acc, mu, carry):
        prev = pltpu.roll(acc, 1, 0)
        row = lax.broadcasted_iota(jnp.int32, acc.shape, 0)
        prev = jnp.where(row == 0, carry, prev)
        return acc + (prev - acc) * mu

    @pl.when(j == 0)
    def _():
        x = h_ref[...]
        ms = jnp.mean(x * x, axis=-1, keepdims=True)
        u = x * lax.rsqrt(ms + NORM_EPS) * prew_ref[...]
        u_ref[...] = u.astype(BF16)
        uh, ul = _split(u)
        d = functools.partial(jnp.dot, preferred_element_type=F32)
        zl = d(uh, wlh_ref[...]) + d(ul, wlh_ref[...]) + d(uh, wll_ref[...])
        zl_ref[...] = shift_mix(zl, mul_ref[...], 0.0)

    def chunked(epilogue):
        accs = {}
        for c in range(IN_ROW_CHUNKS + 1):
            if c < IN_ROW_CHUNKS:
                accs[c] = jnp.dot(u_ref[c * rc:(c + 1) * rc, :], w_ref[...], preferred_element_type=F32)
            if c >= 1:
                carry = accs[c - 2][rc - 1:rc, :] if c >= 2 else 0.0
                epilogue(slice((c - 1) * rc, c * rc), accs[c - 1], carry)
                accs.pop(c - 2, None)

    n_q, n_qk = D_MODEL // tn, 2 * D_MODEL // tn
    is_rope = j < n_qk
    is_rwkv = (j >= COL_RWKV // tn) & (j < COL_HGRN // tn)

    @pl.when(is_rope)
    def _():
        scale = jnp.where(j < n_q, Q_SCALE, 1.0).astype(F32)
        lane = lax.broadcasted_iota(jnp.int32, (rc, LANES), 1)

        def epilogue(rs, acc, _):
            cos = cos_ref[rs, :] * scale
            sin = jnp.where(lane < LANES // 2, -sin_ref[rs, :], sin_ref[rs, :]) * scale
            for g in range(tn // LANES):
                x = acc[:, g * LANES:(g + 1) * LANES]
                z_ref[rs, g * LANES:(g + 1) * LANES] = (x * cos + pltpu.roll(x, LANES // 2, 1) * sin).astype(BF16)

        chunked(epilogue)

    @pl.when(is_rwkv)
    def _():
        def epilogue(rs, acc, carry):
            z_ref[rs, :] = shift_mix(acc, mu_ref[...], carry).astype(BF16)

        chunked(epilogue)

    @pl.when(jnp.logical_not(is_rope | is_rwkv))
    def _():
        def epilogue(rs, acc, _):
            z_ref[rs, :] = acc.astype(BF16)

        chunked(epilogue)


def _in_proj(h, pre_w, w_main, wl_hi, wl_lo, cos, sin, mu_main, mu_lora, *, lp):
    t = h.shape[0]
    nb = t // lp
    tn = IN_TN
    return pl.pallas_call(
        functools.partial(_inproj_kernel, tn=tn),
        grid=(nb, Z_W // tn),
        in_specs=[
            pl.BlockSpec((lp, D_MODEL), lambda i, j: (i, 0)),
            pl.BlockSpec((1, D_MODEL), lambda i, j: (0, 0)),
            pl.BlockSpec((D_MODEL, tn), lambda i, j: (0, j)),
            pl.BlockSpec((D_MODEL, RWKV_LORA), lambda i, j: (0, 0)),
            pl.BlockSpec((D_MODEL, RWKV_LORA), lambda i, j: (0, 0)),
            pl.BlockSpec((lp, LANES), lambda i, j: (0, 0)),
            pl.BlockSpec((lp, LANES), lambda i, j: (0, 0)),
            pl.BlockSpec((1, tn), lambda i, j: (0, j)),
            pl.BlockSpec((1, RWKV_LORA), lambda i, j: (0, 0)),
        ],
        out_specs=[
            pl.BlockSpec((lp, tn), lambda i, j: (i, j)),
            pl.BlockSpec((lp, RWKV_LORA), lambda i, j: (i, 0)),
        ],
        out_shape=[jax.ShapeDtypeStruct((t, Z_W), BF16), jax.ShapeDtypeStruct((t, RWKV_LORA), F32)],
        scratch_shapes=[pltpu.VMEM((lp, D_MODEL), BF16)],
        compiler_params=pltpu.CompilerParams(dimension_semantics=("parallel", "arbitrary"),
                                             vmem_limit_bytes=VMEM_LIMIT),
        name="in_proj",
    )(h, pre_w, w_main, wl_hi, wl_lo, cos, sin, mu_main, mu_lora)


def _attn_tiles(lp):
    return [(r0, min(ATT_TQ, lp - r0)) for r0 in range(0, lp, ATT_TQ)]


def _attn_kernel(lq1_ref, lk1_ref, lq2_ref, lk2_ref, nw_ref, q_ref, k_ref, v_ref, g_ref,
                 o_ref, vx_ref, *p_refs, lam_init):
    lp = q_ref.shape[0]
    lam = (jnp.exp(jnp.sum(lq1_ref[...] * lk1_ref[...], keepdims=True))
           - jnp.exp(jnp.sum(lq2_ref[...] * lk2_ref[...], keepdims=True)) + lam_init)
    vx_ref[:, :LANES] = v_ref[...]
    vx_ref[:, LANES:] = jnp.ones((lp, LANES), BF16)
    out_scale = nw_ref[...] * (1.0 - lam_init)

    def layout(r0, tq):
        col = lax.broadcasted_iota(jnp.int32, (2 * tq, tq), 1)
        causal = col <= lax.broadcasted_iota(jnp.int32, (2 * tq, tq), 0) % tq
        if r0 == 0:
            return [(0, tq)], [causal & (col >= PAD)]
        not_pad = lax.broadcasted_iota(jnp.int32, (2 * tq, ROW_TILE), 1) >= PAD
        return [(0, ROW_TILE), (ROW_TILE, r0), (r0, r0 + tq)], [not_pad, None, causal]

    def scores(r0, tq):
        bounds, masks = layout(r0, tq)
        qb = q_ref[r0:r0 + tq, :].astype(F32)
        half1 = (lax.broadcasted_iota(jnp.int32, (tq, LANES), 1) % ATT_HEAD_DIM) < ATT_HEAD_DIM // 2
        qh = jnp.concatenate([jnp.where(half1, qb, 0.0), jnp.where(half1, 0.0, qb)], axis=0).astype(BF16)
        parts = []
        for (c0, c1), mask in zip(bounds, masks):
            s = lax.dot_general(qh, k_ref[c0:c1, :], (((1,), (1,)), ((), ())), preferred_element_type=F32)
            parts.append(s if mask is None else jnp.where(mask, s, NEG_BIG))
        return parts

    def probs(r0, tq, parts, p_ref):
        bounds, _ = layout(r0, tq)
        cols = [s[:, c:c + LANES] for s in parts for c in range(0, s.shape[1], LANES)]
        m = jnp.max(functools.reduce(jnp.maximum, cols), axis=-1, keepdims=True)
        for s, (c0, c1) in zip(parts, bounds):
            p_ref[:, c0:c1] = jnp.exp2((s - m).astype(BF16))

    def weighted(r0, tq, p_ref):
        pvl = jnp.dot(p_ref[...], vx_ref[0:r0 + tq, :], preferred_element_type=F32)
        o12 = pvl[:, :LANES] * (1.0 / pvl[:, LANES:])
        o = o12[:tq] - lam * o12[tq:]
        o = o * lax.rsqrt(jnp.mean(o * o, axis=-1, keepdims=True) + NORM_EPS) * out_scale
        o_ref[r0:r0 + tq, :] = (o * _silu(g_ref[r0:r0 + tq, :].astype(F32))).astype(BF16)

    tiles = _attn_tiles(lp)
    parts_of = {}
    for k in range(len(tiles) + 2):
        if k < len(tiles):
            parts_of[k] = scores(*tiles[k])
        if 0 <= k - 1 < len(tiles):
            probs(*tiles[k - 1], parts_of.pop(k - 1), p_refs[k - 1])
        if 0 <= k - 2 < len(tiles):
            weighted(*tiles[k - 2], p_refs[k - 2])


def _attention(z, lq1, lk1, lq2, lk2, norm_w, *, lp, lam_init):
    t = z.shape[0]
    nb = t // lp
    small = lambda n: pl.BlockSpec((1, n), lambda b, h: (0, 0))
    colblk = lambda base: pl.BlockSpec((lp, LANES), lambda b, h: (b, base + h))
    return pl.pallas_call(
        functools.partial(_attn_kernel, lam_init=lam_init),
        grid=(nb, ATT_HEADS),
        in_specs=[small(ATT_HEAD_DIM)] * 4 + [small(LANES)]
                 + [colblk(0), colblk(ATT_HEADS), colblk(2 * ATT_HEADS), colblk(3 * ATT_HEADS)],
        out_specs=pl.BlockSpec((lp, LANES), lambda b, h: (b, h)),
        out_shape=jax.ShapeDtypeStruct((t, D_MODEL), BF16),
        scratch_shapes=[pltpu.VMEM((lp, 2 * LANES), BF16)]
                       + [pltpu.VMEM((2 * tq, r0 + tq), BF16) for r0, tq in _attn_tiles(lp)],
        compiler_params=pltpu.CompilerParams(dimension_semantics=("parallel", "parallel"),
                                             vmem_limit_bytes=VMEM_LIMIT),
        name="diff_attn",
    )(lq1, lk1, lq2, lk2, norm_w, z, z, z, z)


RWKV_N_IN = 18


def _rwkv_stages(tril_ref, ones_ref, w0_ref, wuh_ref, wul_ref, a0_ref, auh_ref, aul_ref, kk_ref, ka_ref, rk_ref,
                 gnw_ref, gnb_ref, zr_ref, zk_ref, zv_ref, zg_ref, zl_ref, o_ref, ht_ref):
    n = RWKV_N
    d = functools.partial(jnp.dot, preferred_element_type=F32)

    def lora(x, wh_ref, wl_ref):
        xh, xl = _split(x)
        return d(xh, wh_ref[...]) + d(xl, wh_ref[...]) + d(xh, wl_ref[...])

    def head_sum(x):
        r, nblk = x.shape[0], x.shape[1] // LANES
        stacked = jnp.concatenate([x[:, p * LANES:(p + 1) * LANES] for p in range(nblk)], axis=0)
        s = d(stacked.astype(BF16), ones_ref[...])
        return jnp.concatenate([s[p * r:(p + 1) * r] for p in range(nblk)], axis=1)

    lane = lax.broadcasted_iota(jnp.int32, (CHUNK, LANES), 1)
    t_idx = lax.broadcasted_iota(jnp.int32, (CHUNK, LANES), 0)
    lo = lane < n
    eye = jnp.where(lane % n == t_idx, 1.0, 0.0).astype(F32)
    t2 = lax.broadcasted_iota(jnp.int32, (2 * CHUNK, LANES), 0)
    s2 = lax.broadcasted_iota(jnp.int32, (2 * CHUNK, LANES), 1) % n
    tri2 = s2 < jnp.where(t2 < CHUNK, t2, t2 - CHUNK + 1)
    blk = ((lax.broadcasted_iota(jnp.int32, (LANES, LANES), 0) < n)
           == (lax.broadcasted_iota(jnp.int32, (LANES, LANES), 1) < n))
    pairs = range(RWKV_HEADS // 2)
    sls = [slice(p * LANES, (p + 1) * LANES) for p in pairs]

    def bd(x):
        return jnp.concatenate([jnp.where(lo, x, 0.0), jnp.where(lo, 0.0, x)], axis=0).astype(BF16)

    def mm(a, b):
        return d(a.astype(BF16), b)

    nch = o_ref.shape[0] // CHUNK
    chunks = [slice(i * CHUNK, (i + 1) * CHUNK) for i in range(nch)]
    units = [(rs, sl) for rs in chunks for sl in sls]
    idx = range(len(units))
    zl = zl_ref[...]
    y = -(w0_ref[...] + lora(jnp.tanh(zl[:, :n]), wuh_ref, wul_ref))
    softplus = jnp.maximum(y, 0.0) + jnp.log(1.0 + jnp.exp(-jnp.abs(y)))
    logw = -jnp.exp(-softplus - 0.5)
    a_all = _sigmoid(a0_ref[...] + lora(zl[:, n:], auh_ref, aul_ref))
    yield
    lam_i = _dot_exact_lhs(tril_ref[...], logw)
    r_all, k_all, v_all = (ref[...].astype(F32) for ref in (zr_ref, zk_ref, zv_ref))
    kk = k_all * kk_ref[...]
    kk = kk * lax.rsqrt(jnp.maximum(head_sum(kk * kk), 1e-24))
    yield
    k2 = k_all * (1.0 + (a_all - 1.0) * ka_ref[...])
    kb = kk * a_all
    e_ni = jnp.exp(-lam_i)
    rt = r_all * jnp.exp(lam_i)
    at = -kk * jnp.exp(lam_i - logw)
    bt = kb * e_ni
    kt = k2 * e_ni
    yield
    bonus = head_sum(r_all * k2 * rk_ref[...]) * v_all
    lam_end_ =[lam_i[rs.stop - 1:rs.stop, :] for rs in chunks]
    e_end = jnp.concatenate([jnp.exp(lam_end_[c] - lam_i[chunks[c], :]) for c in range(nch)], axis=0)
    kbe = kb * e_end
    k2e = k2 * e_end
    dec_ = [jnp.exp(lam_end_[c][:, sl]) for c in range(nch) for sl in sls]
    yield

    lhs_ = [jnp.concatenate([at[rs, sl], rt[rs, sl]], axis=0) for rs, sl in units]
    m2_ = [_dot_nt(lhs_[u], jnp.concatenate([bd(bt[rs, sl]), bd(kt[rs, sl])], axis=0))
           for u, (rs, sl) in enumerate(units)]
    yield
    mb_ = [jnp.where(tri2, m2_[u][:, :LANES], 0.0) for u in idx]
    mk_ = [jnp.where(tri2, m2_[u][:, LANES:], 0.0) for u in idx]
    av_ = [mm(mk_[u], bd(v_all[rs, sl])) for u, (rs, sl) in enumerate(units)]
    yield
    tinv_ = [eye + mb_[u][:CHUNK] for u in idx]
    p_ = [mm(mb_[u][:CHUNK], bd(mb_[u][:CHUNK])) for u in idx]
    yield
    for _ in range(int(math.log2(CHUNK)) - 2):
        tp_ = [mm(jnp.concatenate([tinv_[u], p_[u]], axis=0), bd(p_[u])) for u in idx]
        tinv_ = [tinv_[u] + tp_[u][:CHUNK] for u in idx]
        p_ = [tp_[u][CHUNK:] for u in idx]
        yield
    tinv_ = [tinv_[u] + mm(tinv_[u], bd(p_[u])) for u in idx]
    yield
    wu_ = [mm(tinv_[u], jnp.concatenate([bd(at[rs, sl]), bd(av_[u][:CHUNK])], axis=1))
           for u, (rs, sl) in enumerate(units)]
    yield
    glhs_ = [jnp.concatenate([wu_[u][:, :LANES], rt[rs, sl]], axis=0).astype(BF16)
             for u, (rs, sl) in enumerate(units)]
    a_rb_ = [mb_[u][CHUNK:].astype(BF16) for u in idx]
    sv_ = [v_all[rs, sl].astype(BF16) for rs, sl in units]
    sk_ = [jnp.concatenate([kbe[rs, sl], k2e[rs, sl]], axis=0).astype(BF16) for rs, sl in units]

    o_chunks = []
    for c in range(nch):
        us = [c * len(sls) + p for p in pairs]
        ht_ = [ht_ref[p] for p in pairs]
        g2_ = [_dot_nt(glhs_[u], ht_[p]) for p, u in zip(pairs, us)]
        yield
        u_ = [g2_[p][:CHUNK] + wu_[u][:, LANES:] for p, u in zip(pairs, us)]
        o_ = [g2_[p][CHUNK:] + d(a_rb_[u], bd(u_[p])) + av_[u][CHUNK:] for p, u in zip(pairs, us)]
        yield
        for p, u in zip(pairs, us):
            upd = _dot_tn(jnp.concatenate([u_[p].astype(BF16), sv_[u]], axis=0), sk_[u])
            ht_ref[p] = jnp.where(blk, ht_[p] * dec_[u] + upd, 0.0)
        o_chunks.append(jnp.concatenate(o_, axis=1))
        yield
    o = jnp.concatenate(o_chunks, axis=0)
    cen = o - head_sum(o) * (1.0 / n)
    var = head_sum(cen * cen) * (1.0 / n)
    o = cen * lax.rsqrt(var + RWKV_GN_EPS) * gnw_ref[...] + gnb_ref[...] + bonus
    o_ref[...] = (o * _silu(zg_ref[...].astype(F32))).astype(BF16)


def _hgrn_plan():
    c = CHUNK
    p = np.zeros((HGRN_LEVELS + 1, c, c), np.float32)
    m = np.zeros((HGRN_LEVELS, c, c), np.float32)
    p[0] = np.tril(np.ones((c, c), np.float32))
    for lvl in range(HGRN_LEVELS):
        b = 1 << lvl
        for t in range(c):
            mid = (t // (2 * b)) * 2 * b + b
            if t >= mid:
                p[lvl + 1, t, mid:t + 1] = 1.0
                m[lvl, t, mid - b:mid] = 1.0
            else:
                p[lvl + 1, t, t + 1:mid] = 1.0
    return p.reshape(-1, c), m


HGRN_N_IN = 8


def _hgrn_stages(p_ref, m_ref, lbraw_ref, nw_ref, zq_ref, zf_ref, zi_ref, zg_ref, o_ref, st_ref, *, layer):
    lbraw = lbraw_ref[...]
    e = jnp.exp(lbraw - jnp.max(lbraw, axis=0, keepdims=True))
    sm = e / jnp.sum(e, axis=0, keepdims=True)
    lb = jnp.zeros((1, lbraw.shape[1]), F32)
    for i in range(1, layer + 1):
        lb = lb + sm[i:i + 1]
    row = lax.broadcasted_iota(jnp.int32, (CHUNK, CHUNK), 0)
    col = lax.broadcasted_iota(jnp.int32, (CHUNK, CHUNK), 1)
    eye = row == col
    heads = range(HGRN_HEADS)
    sls = [slice(h * HGRN_K, (h + 1) * HGRN_K) for h in heads]

    nch = o_ref.shape[0] // CHUNK
    chunks = [slice(i * CHUNK, (i + 1) * CHUNK) for i in range(nch)]
    units = [(c, sl) for c in range(nch) for sl in sls]
    idx = range(len(units))
    f = lb + (1.0 - lb) * _sigmoid(zf_ref[...].astype(F32))
    logf = jnp.log(f)
    yield
    x_ = [_dot_exact_lhs(p_ref[...], logf[rs, :]) for rs in chunks]
    yield
    e_lvl_ = [jnp.exp(x[CHUNK:]) for x in x_]
    q_all = _silu(zq_ref[...].astype(F32))
    k_all = 1.0 - f
    yield
    q_ = [q_all[chunks[c], sl] for c, sl in units]
    k_ = [k_all[chunks[c], sl] for c, sl in units]
    v_ = [zi_ref[chunks[c], sl] for c, sl in units]
    lam_ = [x_[c][:CHUNK, sl] for c, sl in units]
    att_ = [jnp.where(eye, jnp.sum(q_[u] * k_[u], axis=-1, keepdims=True), 0.0) for u in idx]
    qb_ = [q_[u].astype(BF16) for u in idx]
    kb_ = [k_[u].astype(BF16) for u in idx]
    for lvl in range(HGRN_LEVELS):
        el_ = [e_lvl_[c][lvl * CHUNK:(lvl + 1) * CHUNK, sl].astype(BF16) for c, sl in units]
        att_ = [att_[u] + m_ref[lvl] * _dot_nt(qb_[u] * el_[u], kb_[u] * el_[u]) for u in idx]
        yield
    intra_ = [_dot(att_[u], v_[u]) for u in idx]
    yield
    qe_ = [q_[u] * jnp.exp(lam_[u]) for u in idx]
    lam_end_ = [lam_[u][CHUNK - 1:CHUNK, :] for u in idx]
    kd_ = [k_[u] * jnp.exp(lam_end_[u] - lam_[u]) for u in idx]
    dec_ = [jnp.exp(lam_end_[u]) for u in idx]
    yield

    for c in range(nch):
        us = [c * HGRN_HEADS + h for h in heads]
        st_ = [st_ref[h] for h in heads]
        o_ = [intra_[u] + _dot_nt(qe_[u], st_[h]) for h, u in zip(heads, us)]
        yield
        for h, u in zip(heads, us):
            st_ref[h] = st_[h] * dec_[u] + _dot_tn(v_[u], kd_[u])
        yield
        for h in heads:
            o = o_[h]
            o = o * lax.rsqrt(jnp.mean(o * o, axis=-1, keepdims=True) + NORM_EPS) * nw_ref[...]
            o_ref[chunks[c], sls[h]] = (o * _silu(zg_ref[chunks[c], sls[h]].astype(F32))).astype(BF16)
        yield


def _recur_kernel(*refs, layer):
    rw_in, hg_in = refs[:RWKV_N_IN], refs[RWKV_N_IN:RWKV_N_IN + HGRN_N_IN]
    o_rw_ref, o_hg_ref, ht_ref, st_ref = refs[RWKV_N_IN + HGRN_N_IN:]

    @pl.when(pl.program_id(1) == 0)
    def _():
        ht_ref[...] = jnp.zeros_like(ht_ref)
        st_ref[...] = jnp.zeros_like(st_ref)

    running = [_rwkv_stages(*rw_in, o_rw_ref, ht_ref), _hgrn_stages(*hg_in, o_hg_ref, st_ref, layer=layer)]
    done = object()
    while running:
        running = [g for g in running if next(g, done) is not done]


def _recurrences(z, zl, rwkv_params, hgrn_params, *, lp, layer):
    t = z.shape[0]
    nb, nc = t // lp, lp // SEQ_ROWS
    const = lambda a: pl.BlockSpec(a.shape, lambda b, c: (0,) * a.ndim)
    zblk = lambda col: pl.BlockSpec((SEQ_ROWS, D_MODEL), lambda b, c: (b * nc + c, col // D_MODEL))
    rows = lambda w: pl.BlockSpec((SEQ_ROWS, w), lambda b, c: (b * nc + c, 0))
    rw_cols = [COL_RWKV + k * D_MODEL for k in range(4)]
    hg_cols = [COL_HGRN + k * D_MODEL for k in range(4)]
    in_specs = ([const(a) for a in rwkv_params] + [zblk(c) for c in rw_cols] + [rows(RWKV_LORA)]
                + [const(a) for a in hgrn_params] + [zblk(c) for c in hg_cols])
    assert len(in_specs) == RWKV_N_IN + HGRN_N_IN
    return pl.pallas_call(
        functools.partial(_recur_kernel, layer=layer),
        grid=(nb, nc),
        in_specs=in_specs,
        out_specs=[rows(D_MODEL), rows(D_MODEL)],
        out_shape=[jax.ShapeDtypeStruct((t, D_MODEL), BF16)] * 2,
        scratch_shapes=[pltpu.VMEM((RWKV_HEADS // 2, LANES, LANES), F32),
                        pltpu.VMEM((HGRN_HEADS, HGRN_K, HGRN_K), F32)],
        compiler_params=pltpu.CompilerParams(dimension_semantics=("parallel", "arbitrary"),
                                             vmem_limit_bytes=VMEM_LIMIT),
        name="recurrences",
    )(*rwkv_params, z, z, z, z, zl, *hgrn_params, z, z, z, z)


def _merge_kernel(h_ref, oa_ref, or_ref, oh_ref, ga_ref, gr_ref, gh_ref, wa_ref, wr_ref, wh_ref, wo_ref,
                  postw_ref, out_ref):
    d = functools.partial(jnp.dot, preferred_element_type=F32)
    y = (_sigmoid(ga_ref[...].astype(F32)) * d(oa_ref[...], wa_ref[...])
         + _sigmoid(gr_ref[...].astype(F32)) * d(or_ref[...], wr_ref[...])
         + _sigmoid(gh_ref[...].astype(F32)) * d(oh_ref[...], wh_ref[...]))
    y2 = d(y.astype(BF16), wo_ref[...])
    ms = jnp.mean(y2 * y2, axis=-1, keepdims=True)
    out_ref[...] = h_ref[...] + y2 * lax.rsqrt(ms + NORM_EPS) * postw_ref[...]


def _merge(h, o_att, o_rwkv, o_hgrn, z, w_att, w_rwkv, w_hgrn, w_o, post_w, *, tm):
    t = h.shape[0]
    rowblk = pl.BlockSpec((tm, D_MODEL), lambda i: (i, 0))
    gate = lambda k: pl.BlockSpec((tm, D_MODEL), lambda i: (i, COL_MG // D_MODEL + k))
    wblk = pl.BlockSpec((D_MODEL, D_MODEL), lambda i: (0, 0))
    return pl.pallas_call(
        _merge_kernel,
        grid=(t // tm,),
        in_specs=[rowblk, rowblk, rowblk, rowblk, gate(0), gate(1), gate(2), wblk, wblk, wblk, wblk,
                  pl.BlockSpec((1, D_MODEL), lambda i: (0, 0))],
        out_specs=rowblk,
        out_shape=jax.ShapeDtypeStruct((t, D_MODEL), F32),
        compiler_params=pltpu.CompilerParams(dimension_semantics=("parallel",),
                                             vmem_limit_bytes=VMEM_LIMIT),
        name="merge",
    )(h, o_att, o_rwkv, o_hgrn, z, z, z, w_att, w_rwkv, w_hgrn, w_o, post_w)


def _merge_last(h, o_att, o_rwkv, o_hgrn, z, w_att, w_rwkv, w_hgrn, w_o, post_w, *, lp, tm):
    t = h.shape[0]
    nb, seq = t // lp, lp - ROW_TILE
    nt = seq // tm
    first = lambda b, i: pl.multiple_of(b * lp + ROW_TILE + i * tm, ROW_TILE)
    elem = (pl.Element(tm), pl.Element(D_MODEL))
    rowblk = pl.BlockSpec(elem, lambda b, i: (first(b, i), 0))
    gate = lambda k: pl.BlockSpec(elem, lambda b, i: (first(b, i), COL_MG + k * D_MODEL))
    wblk = pl.BlockSpec((D_MODEL, D_MODEL), lambda b, i: (0, 0))
    return pl.pallas_call(
        _merge_kernel,
        grid=(nb, nt),
        in_specs=[rowblk, rowblk, rowblk, rowblk, gate(0), gate(1), gate(2), wblk, wblk, wblk, wblk,
                  pl.BlockSpec((1, D_MODEL), lambda b, i: (0, 0))],
        out_specs=pl.BlockSpec((tm, D_MODEL), lambda b, i: (b * nt + i, 0)),
        out_shape=jax.ShapeDtypeStruct((nb * seq, D_MODEL), F32),
        compiler_params=pltpu.CompilerParams(dimension_semantics=("parallel", "parallel"),
                                             vmem_limit_bytes=VMEM_LIMIT),
        name="merge_last",
    )(h, o_att, o_rwkv, o_hgrn, z, z, z, w_att, w_rwkv, w_hgrn, w_o, post_w)


def _rotary_tables(lp):
    half = ATT_HEAD_DIM // 2
    inv = 1.0 / (ROPE_THETA ** (jnp.arange(0, ATT_HEAD_DIM, 2, dtype=F32) / ATT_HEAD_DIM))
    pos = jnp.maximum(jnp.arange(lp, dtype=F32) - PAD, 0.0)
    ang = pos[:, None] * inv[None, :]
    ang = jnp.concatenate([ang] * (LANES // half), axis=-1)
    return jnp.cos(ang), jnp.sin(ang)


def kernel(x, meta_tokens, pre_norm_w, post_norm_w, w_in, lambda_q1, lambda_k1, lambda_q2, lambda_k2, att_norm_w, rwkv_mu, rwkv_w0, rwkv_w_up, rwkv_a0, rwkv_a_up, rwkv_k_k, rwkv_k_a, rwkv_r_k, rwkv_gn_w, rwkv_gn_b, hgrn_lower_bounds, hgrn_norm_w, w_att_out, w_rwkv_out, w_hgrn_out, w_o):
    nb, seq, d = x.shape
    depth = w_in.shape[0]
    assert d == D_MODEL and seq % ATT_TQ == 0 and meta_tokens.shape == (N_META, D_MODEL)
    lp = seq + ROW_TILE
    t = nb * lp
    lora0 = COL_HGRN
    lora1 = lora0 + RWKV_LORA

    meta = jnp.broadcast_to(meta_tokens[None].astype(F32), (nb, N_META, D_MODEL))
    h = jnp.concatenate([jnp.zeros((nb, PAD, D_MODEL), F32), meta, x.astype(F32)], axis=1).reshape(t, D_MODEL)
    cos, sin = _rotary_tables(lp)
    tril = jnp.asarray(np.kron(np.eye(SEQ_ROWS // CHUNK), np.tril(np.ones((CHUNK, CHUNK)))), BF16)
    head_of_lane = np.arange(LANES) // RWKV_N
    ones_bd = jnp.asarray(head_of_lane[:, None] == head_of_lane[None, :], BF16)
    pmat_np, masks_np = _hgrn_plan()
    pmat, masks = jnp.asarray(pmat_np, BF16), jnp.asarray(masks_np, F32)
    row = lambda v: v.reshape(1, -1).astype(F32)

    for l in range(depth):
        half = ATT_HEAD_DIM // 2
        w_qk = w_in[l, :, :2 * D_MODEL].reshape(D_MODEL, -1, 2, 2, half).swapaxes(2, 3).reshape(D_MODEL, -1)
        w_main = jnp.concatenate([w_qk, w_in[l, :, 2 * D_MODEL:lora0], w_in[l, :, lora1:]], axis=1).astype(BF16)
        wl_hi, wl_lo = _split(w_in[l, :, lora0:lora1].astype(F32))
        mu = rwkv_mu[l].astype(F32)
        mu_main = jnp.zeros((1, Z_W), F32).at[0, COL_RWKV:COL_HGRN].set(mu[:4 * D_MODEL])
        z, zl = _in_proj(h, row(pre_norm_w[l]), w_main, wl_hi, wl_lo, cos, sin, mu_main,
                         row(mu[4 * D_MODEL:]), lp=lp)
        lam_init = 0.8 - 0.6 * math.exp(-0.3 * l)
        o_att = _attention(z, row(lambda_q1[l]), row(lambda_k1[l]), row(lambda_q2[l]), row(lambda_k2[l]),
                           row(att_norm_w[l]), lp=lp, lam_init=lam_init)
        rwkv_params = [tril, ones_bd, row(rwkv_w0[l]), *_split(rwkv_w_up[l].astype(F32)), row(rwkv_a0[l]),
                       *_split(rwkv_a_up[l].astype(F32)), row(rwkv_k_k[l]), row(rwkv_k_a[l]), row(rwkv_r_k[l]),
                       row(rwkv_gn_w[l]), row(rwkv_gn_b[l])]
        hgrn_params = [pmat, masks, hgrn_lower_bounds.astype(F32), row(hgrn_norm_w[l])]
        o_rwkv, o_hgrn = _recurrences(z, zl, rwkv_params, hgrn_params, lp=lp, layer=l)
        merge_args = (h, o_att, o_rwkv, o_hgrn, z, w_att_out[l].astype(BF16), w_rwkv_out[l].astype(BF16),
                      w_hgrn_out[l].astype(BF16), w_o[l].astype(BF16), row(post_norm_w[l]))
        if l + 1 < depth:
            h = _merge(*merge_args, tm=lp // 4)
        else:
            h = _merge_last(*merge_args, lp=lp, tm=2 * ATT_TQ)
    return h.reshape(nb, seq, D_MODEL).astype(x.dtype)
```

```python
import functools
import math

import numpy as np
import jax
import jax.numpy as jnp
from jax import lax
from jax.experimental import pallas as pl
from jax.experimental.pallas import tpu as pltpu

F32 = jnp.float32
BF16 = jnp.bfloat16

D_MODEL = 1024
N_META = 16
LANES = 128
ROW_TILE = 128
PAD = ROW_TILE - N_META
ROPE_THETA = 10000.0
NORM_EPS = 1e-6
NEG_BIG = -1e30

ATT_HEAD_DIM = 64
ATT_HEADS = 8
ATT_TQ = 256
ATT_HPS = 2
Q_SCALE = math.log2(math.e) * ATT_HEAD_DIM ** -0.5
RWKV_HEADS = 16
RWKV_N = 64
RWKV_GN_EPS = 64e-5
RWKV_LORA = 128
HGRN_HEADS = 8
HGRN_K = 128
CHUNK = 64
SEQ_ROWS = 2 * CHUNK
HGRN_LEVELS = 6

Z_W = 15 * D_MODEL
COL_ATT, COL_RWKV, COL_HGRN, COL_MG = 0, 4 * D_MODEL, 8 * D_MODEL, 12 * D_MODEL
IN_TN = 1024
IN_ROW_CHUNKS = 4
VMEM_LIMIT = 56 * 1024 * 1024


def _dot(a, b):
    return jnp.dot(a.astype(BF16), b.astype(BF16), preferred_element_type=F32)


def _dot_nt(a, b):
    return lax.dot_general(a.astype(BF16), b.astype(BF16), (((1,), (1,)), ((), ())),
                           preferred_element_type=F32)


def _dot_tn(a, b):
    return lax.dot_general(a.astype(BF16), b.astype(BF16), (((0,), (0,)), ((), ())),
                           preferred_element_type=F32)


def _split(x):
    hi = x.astype(BF16)
    lo = (x - hi.astype(F32)).astype(BF16)
    return hi, lo


def _dot_x3(a, b):
    ah, al = _split(a)
    bh, bl = _split(b)
    d = functools.partial(jnp.dot, preferred_element_type=F32)
    return d(ah, bh) + d(al, bh) + d(ah, bl)


def _dot_exact_lhs(p_bf16, x):
    xh, xl = _split(x)
    d = functools.partial(jnp.dot, preferred_element_type=F32)
    return d(p_bf16, xh) + d(p_bf16, xl)


def _sigmoid(x):
    return 1.0 / (1.0 + jnp.exp(-x))


def _silu(x):
    return x * _sigmoid(x)


def _inproj_kernel(h_ref, prew_ref, w_ref, wlh_ref, wll_ref, cos_ref, sin_ref, mu_ref, mul_ref,
                   z_ref, zl_ref, u_ref, *, tn):
    j = pl.program_id(1)
    rows = h_ref.shape[0]

    rc = rows // IN_ROW_CHUNKS

    def shift_mix(acc, mu, carry):
        prev = pltpu.roll(acc, 1, 0)
        row = lax.broadcasted_iota(jnp.int32, acc.shape, 0)
        prev = jnp.where(row == 0, carry, prev)
        return acc + (prev - acc) * mu

    @pl.when(j == 0)
    def _():
        x = h_ref[...]
        ms = jnp.mean(x * x, axis=-1, keepdims=True)
        u = x * lax.rsqrt(ms + NORM_EPS) * prew_ref[...]
        u_ref[...] = u.astype(BF16)
        uh, ul = _split(u)
        d = functools.partial(jnp.dot, preferred_element_type=F32)
        zl = d(uh, wlh_ref[...]) + d(ul, wlh_ref[...]) + d(uh, wll_ref[...])
        zl_ref[...] = shift_mix(zl, mul_ref[...], 0.0)

    def chunked(epilogue):
        accs = {}
        for c in range(IN_ROW_CHUNKS + 1):
            if c < IN_ROW_CHUNKS:
                accs[c] = jnp.dot(u_ref[c * rc:(c + 1) * rc, :], w_ref[...], preferred_element_type=F32)
            if c >= 1:
                carry = accs[c - 2][rc - 1:rc, :] if c >= 2 else 0.0
                epilogue(slice((c - 1) * rc, c * rc), accs[c - 1], carry)
                accs.pop(c - 2, None)

    n_q, n_qk = D_MODEL // tn, 2 * D_MODEL // tn
    is_rope = j < n_qk
    is_rwkv = (j >= COL_RWKV // tn) & (j < COL_HGRN // tn)

    @pl.when(is_rope)
    def _():
        scale = jnp.where(j < n_q, Q_SCALE, 1.0).astype(F32)
        lane = lax.broadcasted_iota(jnp.int32, (rc, LANES), 1)

        def epilogue(rs, acc, _):
            cos = cos_ref[rs, :] * scale
            sin = jnp.where(lane < LANES // 2, -sin_ref[rs, :], sin_ref[rs, :]) * scale
            for g in range(tn // LANES):
                x = acc[:, g * LANES:(g + 1) * LANES]
                z_ref[rs, g * LANES:(g + 1) * LANES] = (x * cos + pltpu.roll(x, LANES // 2, 1) * sin).astype(BF16)

        chunked(epilogue)

    @pl.when(is_rwkv)
    def _():
        def epilogue(rs, acc, carry):
            z_ref[rs, :] = shift_mix(acc, mu_ref[...], carry).astype(BF16)

        chunked(epilogue)

    @pl.when(jnp.logical_not(is_rope | is_rwkv))
    def _():
        def epilogue(rs, acc, _):
            z_ref[rs, :] = acc.astype(BF16)

        chunked(epilogue)


def _in_proj(h, pre_w, w_main, wl_hi, wl_lo, cos, sin, mu_main, mu_lora, *, lp):
    t = h.shape[0]
    nb = t // lp
    tn = IN_TN
    return pl.pallas_call(
        functools.partial(_inproj_kernel, tn=tn),
        grid=(nb, Z_W // tn),
        in_specs=[
            pl.BlockSpec((lp, D_MODEL), lambda i, j: (i, 0)),
            pl.BlockSpec((1, D_MODEL), lambda i, j: (0, 0)),
            pl.BlockSpec((D_MODEL, tn), lambda i, j: (0, j)),
            pl.BlockSpec((D_MODEL, RWKV_LORA), lambda i, j: (0, 0)),
            pl.BlockSpec((D_MODEL, RWKV_LORA), lambda i, j: (0, 0)),
            pl.BlockSpec((lp, LANES), lambda i, j: (0, 0)),
            pl.BlockSpec((lp, LANES), lambda i, j: (0, 0)),
            pl.BlockSpec((1, tn), lambda i, j: (0, j)),
            pl.BlockSpec((1, RWKV_LORA), lambda i, j: (0, 0)),
        ],
        out_specs=[
            pl.BlockSpec((lp, tn), lambda i, j: (i, j)),
            pl.BlockSpec((lp, RWKV_LORA), lambda i, j: (i, 0)),
        ],
        out_shape=[jax.ShapeDtypeStruct((t, Z_W), BF16), jax.ShapeDtypeStruct((t, RWKV_LORA), F32)],
        scratch_shapes=[pltpu.VMEM((lp, D_MODEL), BF16)],
        compiler_params=pltpu.CompilerParams(dimension_semantics=("parallel", "arbitrary"),
                                             vmem_limit_bytes=VMEM_LIMIT),
        name="in_proj",
    )(h, pre_w, w_main, wl_hi, wl_lo, cos, sin, mu_main, mu_lora)


def _attn_tiles(lp):
    return [(r0, min(ATT_TQ, lp - r0)) for r0 in range(0, lp, ATT_TQ)]


def _attn_kernel(lq1_ref, lk1_ref, lq2_ref, lk2_ref, nw_ref, q_ref, k_ref, v_ref, g_ref,
                 o_ref, vx_ref, *p_refs, lam_init):
    lp = q_ref.shape[0]
    lam = (jnp.exp(jnp.sum(lq1_ref[...] * lk1_ref[...], keepdims=True))
           - jnp.exp(jnp.sum(lq2_ref[...] * lk2_ref[...], keepdims=True)) + lam_init)
    cols_of = [slice(hh * LANES, (hh + 1) * LANES) for hh in range(ATT_HPS)]
    for hh, cs in enumerate(cols_of):
        vx_ref[hh, :, :LANES] = v_ref[:, cs]
        vx_ref[hh, :, LANES:] = jnp.ones((lp, LANES), BF16)
    out_scale = nw_ref[...] * (1.0 - lam_init)

    def layout(r0, tq):
        col = lax.broadcasted_iota(jnp.int32, (2 * tq, tq), 1)
        causal = col <= lax.broadcasted_iota(jnp.int32, (2 * tq, tq), 0) % tq
        if r0 == 0:
            return [(0, tq)], [causal & (col >= PAD)]
        not_pad = lax.broadcasted_iota(jnp.int32, (2 * tq, ROW_TILE), 1) >= PAD
        return [(0, ROW_TILE), (ROW_TILE, r0), (r0, r0 + tq)], [not_pad, None, causal]

    def scores(hh, r0, tq):
        bounds, masks = layout(r0, tq)
        qb = q_ref[r0:r0 + tq, cols_of[hh]].astype(F32)
        half1 = (lax.broadcasted_iota(jnp.int32, (tq, LANES), 1) % ATT_HEAD_DIM) < ATT_HEAD_DIM // 2
        qh = jnp.concatenate([jnp.where(half1, qb, 0.0), jnp.where(half1, 0.0, qb)], axis=0).astype(BF16)
        parts = []
        for (c0, c1), mask in zip(bounds, masks):
            s = lax.dot_general(qh, k_ref[c0:c1, cols_of[hh]], (((1,), (1,)), ((), ())),
                                preferred_element_type=F32)
            parts.append(s if mask is None else jnp.where(mask, s, NEG_BIG))
        return parts

    def probs(hh, r0, tq, parts, p_ref):
        bounds, _ = layout(r0, tq)
        cols = [s[:, c:c + LANES] for s in parts for c in range(0, s.shape[1], LANES)]
        m = jnp.max(functools.reduce(jnp.maximum, cols), axis=-1, keepdims=True)
        for s, (c0, c1) in zip(parts, bounds):
            p_ref[:, c0:c1] = jnp.exp2((s - m).astype(BF16))

    def weighted(hh, r0, tq, p_ref):
        pvl = jnp.dot(p_ref[...], vx_ref[hh, 0:r0 + tq, :], preferred_element_type=F32)
        o12 = pvl[:, :LANES] * (1.0 / pvl[:, LANES:])
        o = o12[:tq] - lam * o12[tq:]
        o = o * lax.rsqrt(jnp.mean(o * o, axis=-1, keepdims=True) + NORM_EPS) * out_scale
        gate = g_ref[r0:r0 + tq, cols_of[hh]].astype(F32)
        o_ref[r0:r0 + tq, cols_of[hh]] = (o * _silu(gate)).astype(BF16)

    units = [(hh, r0, tq) for r0, tq in _attn_tiles(lp) for hh in range(ATT_HPS)]
    parts_of = {}
    for k in range(len(units) + 2):
        if k < len(units):
            parts_of[k] = scores(*units[k])
        if 0 <= k - 1 < len(units):
            probs(*units[k - 1], parts_of.pop(k - 1), p_refs[k - 1])
        if 0 <= k - 2 < len(units):
            weighted(*units[k - 2], p_refs[k - 2])


def _attention(z, lq1, lk1, lq2, lk2, norm_w, *, lp, lam_init):
    t = z.shape[0]
    nb = t // lp
    width, groups = ATT_HPS * LANES, ATT_HEADS // ATT_HPS
    small = lambda n: pl.BlockSpec((1, n), lambda b, h: (0, 0))
    colblk = lambda k: pl.BlockSpec((lp, width), lambda b, h: (b, k * groups + h))
    return pl.pallas_call(
        functools.partial(_attn_kernel, lam_init=lam_init),
        grid=(nb, groups),
        in_specs=[small(ATT_HEAD_DIM)] * 4 + [small(LANES)] + [colblk(0), colblk(1), colblk(2), colblk(3)],
        out_specs=pl.BlockSpec((lp, width), lambda b, h: (b, h)),
        out_shape=jax.ShapeDtypeStruct((t, D_MODEL), BF16),
        scratch_shapes=[pltpu.VMEM((ATT_HPS, lp, 2 * LANES), BF16)]
                       + [pltpu.VMEM((2 * tq, r0 + tq), BF16) for r0, tq in _attn_tiles(lp) for _ in range(ATT_HPS)],
        compiler_params=pltpu.CompilerParams(dimension_semantics=("parallel", "parallel"),
                                             vmem_limit_bytes=VMEM_LIMIT),
        name="diff_attn",
    )(lq1, lk1, lq2, lk2, norm_w, z, z, z, z)


RWKV_N_IN = 18


def _rwkv_stages(tril_ref, ones_ref, w0_ref, wuh_ref, wul_ref, a0_ref, auh_ref, aul_ref, kk_ref, ka_ref, rk_ref,
                 gnw_ref, gnb_ref, zr_ref, zk_ref, zv_ref, zg_ref, zl_ref, o_ref, ht_ref):
    n = RWKV_N
    d = functools.partial(jnp.dot, preferred_element_type=F32)

    def lora(x, wh_ref, wl_ref):
        xh, xl = _split(x)
        return d(xh, wh_ref[...]) + d(xl, wh_ref[...]) + d(xh, wl_ref[...])

    def head_sum(x):
        r, nblk = x.shape[0], x.shape[1] // LANES
        stacked = jnp.concatenate([x[:, p * LANES:(p + 1) * LANES] for p in range(nblk)], axis=0)
        s = d(stacked.astype(BF16), ones_ref[...])
        return jnp.concatenate([s[p * r:(p + 1) * r] for p in range(nblk)], axis=1)

    lane = lax.broadcasted_iota(jnp.int32, (CHUNK, LANES), 1)
    t_idx = lax.broadcasted_iota(jnp.int32, (CHUNK, LANES), 0)
    lo = lane < n
    eye = jnp.where(lane % n == t_idx, 1.0, 0.0).astype(F32)
    t2 = lax.broadcasted_iota(jnp.int32, (2 * CHUNK, LANES), 0)
    s2 = lax.broadcasted_iota(jnp.int32, (2 * CHUNK, LANES), 1) % n
    tri2 = s2 < jnp.where(t2 < CHUNK, t2, t2 - CHUNK + 1)
    blk = ((lax.broadcasted_iota(jnp.int32, (LANES, LANES), 0) < n)
           == (lax.broadcasted_iota(jnp.int32, (LANES, LANES), 1) < n))
    pairs = range(RWKV_HEADS // 2)
    sls = [slice(p * LANES, (p + 1) * LANES) for p in pairs]

    def bd(x):
        return jnp.concatenate([jnp.where(lo, x, 0.0), jnp.where(lo, 0.0, x)], axis=0).astype(BF16)

    def mm(a, b):
        return d(a.astype(BF16), b)

    nch = o_ref.shape[0] // CHUNK
    chunks = [slice(i * CHUNK, (i + 1) * CHUNK) for i in range(nch)]
    units = [(rs, sl) for rs in chunks for sl in sls]
    idx = range(len(units))
    zl = zl_ref[...]
    y = -(w0_ref[...] + lora(jnp.tanh(zl[:, :n]), wuh_ref, wul_ref))
    softplus = jnp.maximum(y, 0.0) + jnp.log(1.0 + jnp.exp(-jnp.abs(y)))
    logw = -jnp.exp(-softplus - 0.5)
    a_all = _sigmoid(a0_ref[...] + lora(zl[:, n:], auh_ref, aul_ref))
    yield
    lam_i = _dot_exact_lhs(tril_ref[...], logw)
    r_all, k_all, v_all = (ref[...].astype(F32) for ref in (zr_ref, zk_ref, zv_ref))
    kk = k_all * kk_ref[...]
    kk = kk * lax.rsqrt(jnp.maximum(head_sum(kk * kk), 1e-24))
    yield
    k2 = k_all * (1.0 + (a_all - 1.0) * ka_ref[...])
    kb = kk * a_all
    e_ni = jnp.exp(-lam_i)
    rt = r_all * jnp.exp(lam_i)
    at = -kk * jnp.exp(lam_i - logw)
    bt = kb * e_ni
    kt = k2 * e_ni
    yield
    bonus = head_sum(r_all * k2 * rk_ref[...]) * v_all
    lam_end_ =[lam_i[rs.stop - 1:rs.stop, :] for rs in chunks]
    e_end = jnp.concatenate([jnp.exp(lam_end_[c] - lam_i[chunks[c], :]) for c in range(nch)], axis=0)
    kbe = kb * e_end
    k2e = k2 * e_end
    dec_ = [jnp.exp(lam_end_[c][:, sl]) for c in range(nch) for sl in sls]
    yield

    lhs_ = [jnp.concatenate([at[rs, sl], rt[rs, sl]], axis=0) for rs, sl in units]
    m2_ = [_dot_nt(lhs_[u], jnp.concatenate([bd(bt[rs, sl]), bd(kt[rs, sl])], axis=0))
           for u, (rs, sl) in enumerate(units)]
    yield
    mb_ = [jnp.where(tri2, m2_[u][:, :LANES], 0.0) for u in idx]
    mk_ = [jnp.where(tri2, m2_[u][:, LANES:], 0.0) for u in idx]
    av_ = [mm(mk_[u], bd(v_all[rs, sl])) for u, (rs, sl) in enumerate(units)]
    yield
    tinv_ = [eye + mb_[u][:CHUNK] for u in idx]
    p_ = [mm(mb_[u][:CHUNK], bd(mb_[u][:CHUNK])) for u in idx]
    yield
    for _ in range(int(math.log2(CHUNK)) - 2):
        tp_ = [mm(jnp.concatenate([tinv_[u], p_[u]], axis=0), bd(p_[u])) for u in idx]
        tinv_ = [tinv_[u] + tp_[u][:CHUNK] for u in idx]
        p_ = [tp_[u][CHUNK:] for u in idx]
        yield
    tinv_ = [tinv_[u] + mm(tinv_[u], bd(p_[u])) for u in idx]
    yield
    wu_ = [mm(tinv_[u], jnp.concatenate([bd(at[rs, sl]), bd(av_[u][:CHUNK])], axis=1))
           for u, (rs, sl) in enumerate(units)]
    yield
    glhs_ = [jnp.concatenate([wu_[u][:, :LANES], rt[rs, sl]], axis=0).astype(BF16)
             for u, (rs, sl) in enumerate(units)]
    a_rb_ = [mb_[u][CHUNK:].astype(BF16) for u in idx]
    sv_ = [v_all[rs, sl].astype(BF16) for rs, sl in units]
    sk_ = [jnp.concatenate([kbe[rs, sl], k2e[rs, sl]], axis=0).astype(BF16) for rs, sl in units]

    o_chunks = []
    for c in range(nch):
        us = [c * len(sls) + p for p in pairs]
        ht_ = [ht_ref[p] for p in pairs]
        g2_ = [_dot_nt(glhs_[u], ht_[p]) for p, u in zip(pairs, us)]
        yield
        u_ = [g2_[p][:CHUNK] + wu_[u][:, LANES:] for p, u in zip(pairs, us)]
        o_ = [g2_[p][CHUNK:] + d(a_rb_[u], bd(u_[p])) + av_[u][CHUNK:] for p, u in zip(pairs, us)]
        yield
        for p, u in zip(pairs, us):
            upd = _dot_tn(jnp.concatenate([u_[p].astype(BF16), sv_[u]], axis=0), sk_[u])
            ht_ref[p] = jnp.where(blk, ht_[p] * dec_[u] + upd, 0.0)
        o_chunks.append(jnp.concatenate(o_, axis=1))
        yield
    o = jnp.concatenate(o_chunks, axis=0)
    cen = o - head_sum(o) * (1.0 / n)
    var = head_sum(cen * cen) * (1.0 / n)
    o = cen * lax.rsqrt(var + RWKV_GN_EPS) * gnw_ref[...] + gnb_ref[...] + bonus
    o_ref[...] = (o * _silu(zg_ref[...].astype(F32))).astype(BF16)


def _hgrn_plan():
    c = CHUNK
    p = np.zeros((HGRN_LEVELS + 1, c, c), np.float32)
    m = np.zeros((HGRN_LEVELS, c, c), np.float32)
    p[0] = np.tril(np.ones((c, c), np.float32))
    for lvl in range(HGRN_LEVELS):
        b = 1 << lvl
        for t in range(c):
            mid = (t // (2 * b)) * 2 * b + b
            if t >= mid:
                p[lvl + 1, t, mid:t + 1] = 1.0
                m[lvl, t, mid - b:mid] = 1.0
            else:
                p[lvl + 1, t, t + 1:mid] = 1.0
    return p.reshape(-1, c), m


HGRN_N_IN = 8


def _hgrn_stages(p_ref, m_ref, lbraw_ref, nw_ref, zq_ref, zf_ref, zi_ref, zg_ref, o_ref, st_ref, *, layer):
    lbraw = lbraw_ref[...]
    e = jnp.exp(lbraw - jnp.max(lbraw, axis=0, keepdims=True))
    sm = e / jnp.sum(e, axis=0, keepdims=True)
    lb = jnp.zeros((1, lbraw.shape[1]), F32)
    for i in range(1, layer + 1):
        lb = lb + sm[i:i + 1]
    row = lax.broadcasted_iota(jnp.int32, (CHUNK, CHUNK), 0)
    col = lax.broadcasted_iota(jnp.int32, (CHUNK, CHUNK), 1)
    eye = row == col
    heads = range(HGRN_HEADS)
    sls = [slice(h * HGRN_K, (h + 1) * HGRN_K) for h in heads]

    nch = o_ref.shape[0] // CHUNK
    chunks = [slice(i * CHUNK, (i + 1) * CHUNK) for i in range(nch)]
    units = [(c, sl) for c in range(nch) for sl in sls]
    idx = range(len(units))
    f = lb + (1.0 - lb) * _sigmoid(zf_ref[...].astype(F32))
    logf = jnp.log(f)
    yield
    x_ = [_dot_exact_lhs(p_ref[...], logf[rs, :]) for rs in chunks]
    yield
    e_lvl_ = [jnp.exp(x[CHUNK:]) for x in x_]
    q_all = _silu(zq_ref[...].astype(F32))
    k_all = 1.0 - f
    yield
    q_ = [q_all[chunks[c], sl] for c, sl in units]
    k_ = [k_all[chunks[c], sl] for c, sl in units]
    v_ = [zi_ref[chunks[c], sl] for c, sl in units]
    lam_ = [x_[c][:CHUNK, sl] for c, sl in units]
    att_ = [jnp.where(eye, jnp.sum(q_[u] * k_[u], axis=-1, keepdims=True), 0.0) for u in idx]
    qb_ = [q_[u].astype(BF16) for u in idx]
    kb_ = [k_[u].astype(BF16) for u in idx]
    for lvl in range(HGRN_LEVELS):
        el_ = [e_lvl_[c][lvl * CHUNK:(lvl + 1) * CHUNK, sl].astype(BF16) for c, sl in units]
        att_ = [att_[u] + m_ref[lvl] * _dot_nt(qb_[u] * el_[u], kb_[u] * el_[u]) for u in idx]
        yield
    intra_ = [_dot(att_[u], v_[u]) for u in idx]
    yield
    qe_ = [q_[u] * jnp.exp(lam_[u]) for u in idx]
    lam_end_ = [lam_[u][CHUNK - 1:CHUNK, :] for u in idx]
    kd_ = [k_[u] * jnp.exp(lam_end_[u] - lam_[u]) for u in idx]
    dec_ = [jnp.exp(lam_end_[u]) for u in idx]
    yield

    for c in range(nch):
        us = [c * HGRN_HEADS + h for h in heads]
        st_ = [st_ref[h] for h in heads]
        o_ = [intra_[u] + _dot_nt(qe_[u], st_[h]) for h, u in zip(heads, us)]
        yield
        for h, u in zip(heads, us):
            st_ref[h] = st_[h] * dec_[u] + _dot_tn(v_[u], kd_[u])
        yield
        for h in heads:
            o = o_[h]
            o = o * lax.rsqrt(jnp.mean(o * o, axis=-1, keepdims=True) + NORM_EPS) * nw_ref[...]
            o_ref[chunks[c], sls[h]] = (o * _silu(zg_ref[chunks[c], sls[h]].astype(F32))).astype(BF16)
        yield


def _recur_kernel(*refs, layer):
    rw_in, hg_in = refs[:RWKV_N_IN], refs[RWKV_N_IN:RWKV_N_IN + HGRN_N_IN]
    o_rw_ref, o_hg_ref, ht_ref, st_ref = refs[RWKV_N_IN + HGRN_N_IN:]

    @pl.when(pl.program_id(1) == 0)
    def _():
        ht_ref[...] = jnp.zeros_like(ht_ref)
        st_ref[...] = jnp.zeros_like(st_ref)

    running = [_rwkv_stages(*rw_in, o_rw_ref, ht_ref), _hgrn_stages(*hg_in, o_hg_ref, st_ref, layer=layer)]
    done = object()
    while running:
        running = [g for g in running if next(g, done) is not done]


def _recurrences(z, zl, rwkv_params, hgrn_params, *, lp, layer):
    t = z.shape[0]
    nb, nc = t // lp, lp // SEQ_ROWS
    const = lambda a: pl.BlockSpec(a.shape, lambda b, c: (0,) * a.ndim)
    zblk = lambda col: pl.BlockSpec((SEQ_ROWS, D_MODEL), lambda b, c: (b * nc + c, col // D_MODEL))
    rows = lambda w: pl.BlockSpec((SEQ_ROWS, w), lambda b, c: (b * nc + c, 0))
    rw_cols = [COL_RWKV + k * D_MODEL for k in range(4)]
    hg_cols = [COL_HGRN + k * D_MODEL for k in range(4)]
    in_specs = ([const(a) for a in rwkv_params] + [zblk(c) for c in rw_cols] + [rows(RWKV_LORA)]
                + [const(a) for a in hgrn_params] + [zblk(c) for c in hg_cols])
    assert len(in_specs) == RWKV_N_IN + HGRN_N_IN
    return pl.pallas_call(
        functools.partial(_recur_kernel, layer=layer),
        grid=(nb, nc),
        in_specs=in_specs,
        out_specs=[rows(D_MODEL), rows(D_MODEL)],
        out_shape=[jax.ShapeDtypeStruct((t, D_MODEL), BF16)] * 2,
        scratch_shapes=[pltpu.VMEM((RWKV_HEADS // 2, LANES, LANES), F32),
                        pltpu.VMEM((HGRN_HEADS, HGRN_K, HGRN_K), F32)],
        compiler_params=pltpu.CompilerParams(dimension_semantics=("parallel", "arbitrary"),
                                             vmem_limit_bytes=VMEM_LIMIT),
        name="recurrences",
    )(*rwkv_params, z, z, z, z, zl, *hgrn_params, z, z, z, z)


def _merge_kernel(h_ref, oa_ref, or_ref, oh_ref, ga_ref, gr_ref, gh_ref, wa_ref, wr_ref, wh_ref, wo_ref,
                  postw_ref, out_ref):
    d = functools.partial(jnp.dot, preferred_element_type=F32)
    y = (_sigmoid(ga_ref[...].astype(F32)) * d(oa_ref[...], wa_ref[...])
         + _sigmoid(gr_ref[...].astype(F32)) * d(or_ref[...], wr_ref[...])
         + _sigmoid(gh_ref[...].astype(F32)) * d(oh_ref[...], wh_ref[...]))
    y2 = d(y.astype(BF16), wo_ref[...])
    ms = jnp.mean(y2 * y2, axis=-1, keepdims=True)
    out_ref[...] = h_ref[...] + y2 * lax.rsqrt(ms + NORM_EPS) * postw_ref[...]


def _merge(h, o_att, o_rwkv, o_hgrn, z, w_att, w_rwkv, w_hgrn, w_o, post_w, *, tm):
    t = h.shape[0]
    rowblk = pl.BlockSpec((tm, D_MODEL), lambda i: (i, 0))
    gate = lambda k: pl.BlockSpec((tm, D_MODEL), lambda i: (i, COL_MG // D_MODEL + k))
    wblk = pl.BlockSpec((D_MODEL, D_MODEL), lambda i: (0, 0))
    return pl.pallas_call(
        _merge_kernel,
        grid=(t // tm,),
        in_specs=[rowblk, rowblk, rowblk, rowblk, gate(0), gate(1), gate(2), wblk, wblk, wblk, wblk,
                  pl.BlockSpec((1, D_MODEL), lambda i: (0, 0))],
        out_specs=rowblk,
        out_shape=jax.ShapeDtypeStruct((t, D_MODEL), F32),
        compiler_params=pltpu.CompilerParams(dimension_semantics=("parallel",),
                                             vmem_limit_bytes=VMEM_LIMIT),
        name="merge",
    )(h, o_att, o_rwkv, o_hgrn, z, z, z, w_att, w_rwkv, w_hgrn, w_o, post_w)


def _merge_last(h, o_att, o_rwkv, o_hgrn, z, w_att, w_rwkv, w_hgrn, w_o, post_w, *, lp, tm):
    t = h.shape[0]
    nb, seq = t // lp, lp - ROW_TILE
    nt = seq // tm
    first = lambda b, i: pl.multiple_of(b * lp + ROW_TILE + i * tm, ROW_TILE)
    elem = (pl.Element(tm), pl.Element(D_MODEL))
    rowblk = pl.BlockSpec(elem, lambda b, i: (first(b, i), 0))
    gate = lambda k: pl.BlockSpec(elem, lambda b, i: (first(b, i), COL_MG + k * D_MODEL))
    wblk = pl.BlockSpec((D_MODEL, D_MODEL), lambda b, i: (0, 0))
    return pl.pallas_call(
        _merge_kernel,
        grid=(nb, nt),
        in_specs=[rowblk, rowblk, rowblk, rowblk, gate(0), gate(1), gate(2), wblk, wblk, wblk, wblk,
                  pl.BlockSpec((1, D_MODEL), lambda b, i: (0, 0))],
        out_specs=pl.BlockSpec((tm, D_MODEL), lambda b, i: (b * nt + i, 0)),
        out_shape=jax.ShapeDtypeStruct((nb * seq, D_MODEL), F32),
        compiler_params=pltpu.CompilerParams(dimension_semantics=("parallel", "parallel"),
                                             vmem_limit_bytes=VMEM_LIMIT),
        name="merge_last",
    )(h, o_att, o_rwkv, o_hgrn, z, z, z, w_att, w_rwkv, w_hgrn, w_o, post_w)


def _rotary_tables(lp):
    half = ATT_HEAD_DIM // 2
    inv = 1.0 / (ROPE_THETA ** (jnp.arange(0, ATT_HEAD_DIM, 2, dtype=F32) / ATT_HEAD_DIM))
    pos = jnp.maximum(jnp.arange(lp, dtype=F32) - PAD, 0.0)
    ang = pos[:, None] * inv[None, :]
    ang = jnp.concatenate([ang] * (LANES // half), axis=-1)
    return jnp.cos(ang), jnp.sin(ang)


def kernel(x, meta_tokens, pre_norm_w, post_norm_w, w_in, lambda_q1, lambda_k1, lambda_q2, lambda_k2, att_norm_w, rwkv_mu, rwkv_w0, rwkv_w_up, rwkv_a0, rwkv_a_up, rwkv_k_k, rwkv_k_a, rwkv_r_k, rwkv_gn_w, rwkv_gn_b, hgrn_lower_bounds, hgrn_norm_w, w_att_out, w_rwkv_out, w_hgrn_out, w_o):
    nb, seq, d = x.shape
    depth = w_in.shape[0]
    assert d == D_MODEL and seq % ATT_TQ == 0 and meta_tokens.shape == (N_META, D_MODEL)
    lp = seq + ROW_TILE
    t = nb * lp
    lora0 = COL_HGRN
    lora1 = lora0 + RWKV_LORA

    meta = jnp.broadcast_to(meta_tokens[None].astype(F32), (nb, N_META, D_MODEL))
    h = jnp.concatenate([jnp.zeros((nb, PAD, D_MODEL), F32), meta, x.astype(F32)], axis=1).reshape(t, D_MODEL)
    cos, sin = _rotary_tables(lp)
    tril = jnp.asarray(np.kron(np.eye(SEQ_ROWS // CHUNK), np.tril(np.ones((CHUNK, CHUNK)))), BF16)
    head_of_lane = np.arange(LANES) // RWKV_N
    ones_bd = jnp.asarray(head_of_lane[:, None] == head_of_lane[None, :], BF16)
    pmat_np, masks_np = _hgrn_plan()
    pmat, masks = jnp.asarray(pmat_np, BF16), jnp.asarray(masks_np, F32)
    row = lambda v: v.reshape(1, -1).astype(F32)

    for l in range(depth):
        half = ATT_HEAD_DIM // 2
        w_qk = w_in[l, :, :2 * D_MODEL].reshape(D_MODEL, -1, 2, 2, half).swapaxes(2, 3).reshape(D_MODEL, -1)
        w_main = jnp.concatenate([w_qk, w_in[l, :, 2 * D_MODEL:lora0], w_in[l, :, lora1:]], axis=1).astype(BF16)
        wl_hi, wl_lo = _split(w_in[l, :, lora0:lora1].astype(F32))
        mu = rwkv_mu[l].astype(F32)
        mu_main = jnp.zeros((1, Z_W), F32).at[0, COL_RWKV:COL_HGRN].set(mu[:4 * D_MODEL])
        z, zl = _in_proj(h, row(pre_norm_w[l]), w_main, wl_hi, wl_lo, cos, sin, mu_main,
                         row(mu[4 * D_MODEL:]), lp=lp)
        lam_init = 0.8 - 0.6 * math.exp(-0.3 * l)
        o_att = _attention(z, row(lambda_q1[l]), row(lambda_k1[l]), row(lambda_q2[l]), row(lambda_k2[l]),
                           row(att_norm_w[l]), lp=lp, lam_init=lam_init)
        rwkv_params = [tril, ones_bd, row(rwkv_w0[l]), *_split(rwkv_w_up[l].astype(F32)), row(rwkv_a0[l]),
                       *_split(rwkv_a_up[l].astype(F32)), row(rwkv_k_k[l]), row(rwkv_k_a[l]), row(rwkv_r_k[l]),
                       row(rwkv_gn_w[l]), row(rwkv_gn_b[l])]
        hgrn_params = [pmat, masks, hgrn_lower_bounds.astype(F32), row(hgrn_norm_w[l])]
        o_rwkv, o_hgrn = _recurrences(z, zl, rwkv_params, hgrn_params, lp=lp, layer=l)
        merge_args = (h, o_att, o_rwkv, o_hgrn, z, w_att_out[l].astype(BF16), w_rwkv_out[l].astype(BF16),
                      w_hgrn_out[l].astype(BF16), w_o[l].astype(BF16), row(post_norm_w[l]))
        if l + 1 < depth:
            h = _merge(*merge_args, tm=lp // 4)
        else:
            h = _merge_last(*merge_args, lp=lp, tm=2 * ATT_TQ)
    return h.reshape(nb, seq, D_MODEL).astype(x.dtype)
```
